```python
import jax, jax.numpy as jnp
from jax import lax
import numpy as np

D_MODEL = 1024
BATCH = 8
SEQ = 4096
DEPTH = 4

HGRN_HEADS = 4
HGRN_HEAD_DIM = 128
HGRN_WIDTH = HGRN_HEADS * HGRN_HEAD_DIM
HGRN_CHUNK = 64
SB_HEADS = 8
SB_HEAD_DIM = 64
SB_WIDTH = SB_HEADS * SB_HEAD_DIM
SB_BLOCK = 128
FFN_DIM = 3584
N_EXPERTS = 8
TOP_K = 2
RMS_EPS = 1e-6
IN_WIDTH = 4 * HGRN_WIDTH + 3 * SB_WIDTH + 2 * D_MODEL

kernel_name = 'hybrid_hgrn2_stickbreak_moe'


def _split_points():
    sizes = [HGRN_WIDTH] * 4 + [SB_WIDTH] * 3 + [D_MODEL] * 2
    pts, acc = [], 0
    for s in sizes[:-1]:
        acc += s
        pts.append(acc)
    return pts


def rms_norm(x, g):
    x32 = x.astype(jnp.float32)
    y = x32 * lax.rsqrt(jnp.mean(x32 * x32, axis=-1, keepdims=True) + RMS_EPS)
    return (y * g.astype(jnp.float32)).astype(x.dtype)


def hgrn2_chunkwise(q, k, v, log_f):
    B, S, H, Dk = q.shape
    Dv = v.shape[-1]
    n = S // HGRN_CHUNK

    def to_chunks(t):
        return t.reshape(B, n, HGRN_CHUNK, H, t.shape[-1]).transpose(1, 0, 3, 2, 4)

    causal = jnp.tril(jnp.ones((HGRN_CHUNK, HGRN_CHUNK), dtype=bool))[:, :, None]

    def step(s_prev, inp):
        qb, kb, vb, fb = inp
        b = jnp.cumsum(fb, axis=2)
        diff = b[:, :, :, None, :] - b[:, :, None, :, :]
        decay = jnp.exp(jnp.where(causal, diff, -jnp.inf))
        scores = jnp.einsum('bhtd,bhtsd,bhsd->bhts', qb, decay, kb)
        o = (jnp.einsum('bhts,bhsv->bhtv', scores, vb)
             + jnp.einsum('bhtd,bhdv->bhtv', qb * jnp.exp(b), s_prev))
        b_last = b[:, :, -1:, :]
        s_new = (jnp.exp(b_last[:, :, 0, :])[..., None] * s_prev
                 + jnp.einsum('bhsd,bhsv->bhdv', kb * jnp.exp(b_last - b), vb))
        return s_new, o

    s0 = jnp.zeros((B, H, Dk, Dv), jnp.float32)
    _, o = lax.scan(step, s0, (to_chunks(q), to_chunks(k), to_chunks(v), to_chunks(log_f)))
    return o.transpose(1, 0, 3, 2, 4).reshape(B, S, H, Dv)


def stick_breaking_attention(q, k, v):
    B, S, H, d = q.shape
    scale = d ** -0.5
    outs = []
    for blk in range(S // SB_BLOCK):
        t0 = blk * SB_BLOCK
        kl = t0 + SB_BLOCK
        z = jnp.einsum('bthd,bshd->bhts', q[:, t0:kl], k[:, :kl]).astype(jnp.float32) * scale
        t_idx = t0 + jnp.arange(SB_BLOCK)[:, None]
        s_idx = jnp.arange(kl)[None, :]
        strict = s_idx < t_idx
        log_keep = jnp.where(strict, jax.nn.log_sigmoid(-z), 0.0)
        after = lax.cumsum(log_keep, axis=3, reverse=True) - log_keep
        w = jnp.where(strict, jnp.exp(jax.nn.log_sigmoid(z) + after), 0.0)
        outs.append(jnp.einsum('bhts,bshd->bthd', w, v[:, :kl].astype(jnp.float32)))
    return jnp.concatenate(outs, axis=1)


def hybrid_mixer(h, w_in, lb, out_norm, w_branch_a, w_branch_b, w_out):
    B, S, _ = h.shape
    proj = jnp.einsum('bsd,dn->bsn', h, w_in)
    hq, hf, hi, hog, sq, sk, sv, ga, gb = jnp.split(proj, _split_points(), axis=-1)

    def hh(t):
        return t.reshape(B, S, HGRN_HEADS, HGRN_HEAD_DIM).astype(jnp.float32)
    lb = lb.reshape(HGRN_HEADS, HGRN_HEAD_DIM)
    zf = hh(hf)
    log_f = jnp.logaddexp(jnp.log(lb), jnp.log1p(-lb) + jax.nn.log_sigmoid(zf))
    k_in = (1.0 - lb) * jax.nn.sigmoid(-zf)
    o_a = hgrn2_chunkwise(jax.nn.silu(hh(hq)), k_in, hh(hi), log_f)
    o_a = rms_norm(o_a, out_norm) * jax.nn.silu(hh(hog))
    o_a = o_a.reshape(B, S, HGRN_WIDTH).astype(h.dtype)

    def sh(t):
        return t.reshape(B, S, SB_HEADS, SB_HEAD_DIM)
    o_b = stick_breaking_attention(sh(sq), sh(sk), sh(sv)).reshape(B, S, SB_WIDTH).astype(h.dtype)

    y = (jax.nn.sigmoid(ga) * jnp.einsum('bsn,nd->bsd', o_a, w_branch_a)
         + jax.nn.sigmoid(gb) * jnp.einsum('bsn,nd->bsd', o_b, w_branch_b))
    return jnp.einsum('bsd,de->bse', y, w_out)


def swiglu(h, w_gate, w_up, w_down):
    a = jnp.einsum('...d,df->...f', h, w_gate)
    u = jnp.einsum('...d,df->...f', h, w_up)
    return jnp.einsum('...f,fd->...d', jax.nn.silu(a) * u, w_down)


def moe_swiglu(h, w_router, w_gate, w_up, w_down):
    B, S, D = h.shape
    t = h.reshape(B * S, D)
    logits = jnp.einsum('td,de->te', t, w_router).astype(jnp.float32)
    top_val, top_idx = lax.top_k(logits, TOP_K)
    top_w = jax.nn.softmax(top_val, axis=-1)
    combine = jnp.einsum('tk,tke->te', top_w, jax.nn.one_hot(top_idx, N_EXPERTS, dtype=jnp.float32))
    y = jnp.zeros((B * S, D), jnp.float32)
    for e in range(N_EXPERTS):
        y = y + combine[:, e:e + 1] * swiglu(t, w_gate[e], w_up[e], w_down[e]).astype(jnp.float32)
    return y.astype(h.dtype).reshape(B, S, D)


def setup_inputs(seed: int = 0) -> dict:
    key = jax.random.key(seed)
    ks = jax.random.split(key, 20)
    f32 = jnp.float32
    n_dense = (DEPTH + 1) // 2
    n_moe = DEPTH // 2
    res = (2 * DEPTH) ** -0.5

    def nrm(k, shape, fan_in, extra=1.0):
        return jax.random.normal(k, shape, f32) * (fan_in ** -0.5) * extra

    def gain(k, shape):
        return 1.0 + 0.02 * jax.random.normal(k, shape, f32)

    return {
        'x': jax.random.normal(ks[0], (BATCH, SEQ, D_MODEL), f32),
        'mix_norm': gain(ks[1], (DEPTH, D_MODEL)),
        'w_in': nrm(ks[2], (DEPTH, D_MODEL, IN_WIDTH), D_MODEL),
        'hgrn_lb_logits': 0.5 * jax.random.normal(ks[3], (DEPTH, HGRN_WIDTH), f32),
        'hgrn_out_norm': gain(ks[4], (DEPTH, HGRN_HEAD_DIM)),
        'w_branch_hgrn': nrm(ks[5], (DEPTH, HGRN_WIDTH, D_MODEL), HGRN_WIDTH),
        'w_branch_sb': nrm(ks[6], (DEPTH, SB_WIDTH, D_MODEL), SB_WIDTH),
        'w_out': nrm(ks[7], (DEPTH, D_MODEL, D_MODEL), D_MODEL, res),
        'ffn_norm': gain(ks[8], (DEPTH, D_MODEL)),
        'dense_w_gate': nrm(ks[9], (n_dense, D_MODEL, FFN_DIM), D_MODEL),
        'dense_w_up': nrm(ks[10], (n_dense, D_MODEL, FFN_DIM), D_MODEL),
        'dense_w_down': nrm(ks[11], (n_dense, FFN_DIM, D_MODEL), FFN_DIM, res),
        'moe_router': nrm(ks[12], (n_moe, D_MODEL, N_EXPERTS), D_MODEL),
        'moe_w_gate': nrm(ks[13], (n_moe, N_EXPERTS, D_MODEL, FFN_DIM), D_MODEL),
        'moe_w_up': nrm(ks[14], (n_moe, N_EXPERTS, D_MODEL, FFN_DIM), D_MODEL),
        'moe_w_down': nrm(ks[15], (n_moe, N_EXPERTS, FFN_DIM, D_MODEL), FFN_DIM, res),
        'final_norm': gain(ks[16], (D_MODEL,)),
    }


def reference(x, mix_norm, w_in, hgrn_lb_logits, hgrn_out_norm, w_branch_hgrn, w_branch_sb, w_out,
              ffn_norm, dense_w_gate, dense_w_up, dense_w_down, moe_router, moe_w_gate, moe_w_up,
              moe_w_down, final_norm):
    lb_all = jnp.cumsum(jax.nn.softmax(hgrn_lb_logits.astype(jnp.float32), axis=0), axis=0)
    lb_all = lb_all - lb_all[0:1]
    for layer in range(DEPTH):
        h = rms_norm(x, mix_norm[layer])
        x = x + hybrid_mixer(h, w_in[layer], lb_all[layer], hgrn_out_norm[layer],
                             w_branch_hgrn[layer], w_branch_sb[layer], w_out[layer])
        h = rms_norm(x, ffn_norm[layer])
        j = layer // 2
        if layer % 2 == 0:
            x = x + swiglu(h, dense_w_gate[j], dense_w_up[j], dense_w_down[j])
        else:
            x = x + moe_swiglu(h, moe_router[j], moe_w_gate[j], moe_w_up[j], moe_w_down[j])
    return rms_norm(x, final_norm)
```

```python
import functools

import jax
import jax.numpy as jnp
from jax import lax
from jax.experimental import pallas as pl
from jax.experimental.pallas import tpu as pltpu

F32 = jnp.float32
BF16 = jnp.bfloat16

LANES = 128
HGRN_HEADS = 4
HGRN_HEAD_DIM = 128
SB_HEADS = 8
SB_HEAD_DIM = 64
SB_HEADS_PER_PLANE = LANES // SB_HEAD_DIM
N_EXPERTS = 8
RMS_EPS = 1e-6
VMEM_LIMIT_BYTES = 56 * 1024 * 1024

HGRN_CHUNK = 128
HGRN_DIAG = 16
HGRN_ROWS = 512
SB_BLOCK = 128


def _cparams(semantics):
    return pltpu.CompilerParams(dimension_semantics=semantics, vmem_limit_bytes=VMEM_LIMIT_BYTES)


def _rms(x, g):
    ms = jnp.mean(x * x, axis=-1, keepdims=True)
    return x * lax.rsqrt(ms + RMS_EPS) * g


def _sigmoid(x):
    return 1.0 / (1.0 + jnp.exp(-x))


def _dot(a, b):
    return jnp.dot(a, b, preferred_element_type=F32)


def _dot_nt(a, b):
    return lax.dot_general(a, b, (((1,), (1,)), ((), ())), preferred_element_type=F32)


def _dot_tn(a, b):
    return lax.dot_general(a, b, (((0,), (0,)), ((), ())), preferred_element_type=F32)


def _inproj_kernel(x_ref, g_ref, w_ref, o_ref, h_scr):
    @pl.when(pl.program_id(1) == 0)
    def _():
        h_scr[...] = _rms(x_ref[...], g_ref[...]).astype(BF16)

    r = _dot(h_scr[...], w_ref[...])
    for c in range(o_ref.shape[0]):
        o_ref[c] = r[:, c * LANES:(c + 1) * LANES].astype(BF16)


def _inproj(x2, g, w, *, tm=1024, tn=512):
    t, d = x2.shape
    n = w.shape[1]
    tm = min(tm, t)
    return pl.pallas_call(
        _inproj_kernel,
        grid=(t // tm, n // tn),
        in_specs=[pl.BlockSpec((tm, d), lambda i, j: (i, 0)),
                  pl.BlockSpec((1, d), lambda i, j: (0, 0)),
                  pl.BlockSpec((d, tn), lambda i, j: (0, j))],
        out_specs=pl.BlockSpec((tn // LANES, tm, LANES), lambda i, j: (j, i, 0)),
        out_shape=jax.ShapeDtypeStruct((n // LANES, t, LANES), BF16),
        scratch_shapes=[pltpu.VMEM((tm, d), BF16)],
        compiler_params=_cparams(("parallel", "arbitrary")),
        name="inproj",
    )(x2, g.reshape(1, d), w)


def _hgrn_chunk(zq, zf, vi, zog, lbp, gn, st):
    c = zq.shape[0]
    log_lb, log1m_lb, one_m_lb = lbp[0:1], lbp[1:2], lbp[2:3]

    q = zq * _sigmoid(zq)
    e = jnp.exp(-jnp.abs(zf))
    den = 1.0 + e
    log_sig = jnp.minimum(zf, 0.0) - jnp.log(den)
    k = one_m_lb * (jnp.where(zf >= 0.0, e, 1.0) / den)
    cc = log1m_lb + log_sig
    log_f = jnp.maximum(log_lb, cc) + jnp.log(1.0 + jnp.exp(-jnp.abs(log_lb - cc)))

    hi = log_f.astype(BF16)
    r1 = log_f - hi.astype(F32)
    mid = r1.astype(BF16)
    lo = (r1 - mid.astype(F32)).astype(BF16)
    row = lax.broadcasted_iota(jnp.int32, (c, c), 0)
    col = lax.broadcasted_iota(jnp.int32, (c, c), 1)
    tri = jnp.where(col <= row, 1.0, 0.0).astype(BF16)
    b3 = _dot(tri, jnp.concatenate([hi, mid, lo], axis=1))
    b = b3[:, :LANES] + b3[:, LANES:2 * LANES] + b3[:, 2 * LANES:]
    b_prev = b - log_f
    b_last = b[c - 1:c, :]

    o = _dot_nt((q * jnp.exp(b)).astype(BF16), st.astype(BF16))
    k_tail = (k * jnp.exp(b_last - b)).astype(BF16)
    st_new = st * jnp.exp(b_last) + _dot_tn(vi, k_tail)

    rows = lax.broadcasted_iota(jnp.int32, (c, 1), 0)
    scores = jnp.zeros((c, c), F32)
    m = HGRN_DIAG
    while m < c:
        n = c // m
        base = b_prev.reshape(n, m, LANES)[:, 0:1, :]
        p = b.reshape(n, m, LANES) - base
        suf = (p[:, m - 1:m, :] - p).reshape(c, LANES)
        p = p.reshape(c, LANES)
        odd = ((rows // m) % 2) == 1
        q_m = jnp.where(odd, q * jnp.exp(p), 0.0).astype(BF16)
        k_m = jnp.where(odd, 0.0, k * jnp.exp(suf)).astype(BF16)
        s_m = _dot_nt(q_m, k_m)
        scores = scores + jnp.where((row // (2 * m)) == (col // (2 * m)), s_m, 0.0)
        m *= 2
    o = o + _dot(scores.astype(BF16), vi)

    vf = vi.astype(F32)
    tmod = rows % HGRN_DIAG
    for j in range(HGRN_DIAG):
        if j == 0:
            ks, bs, vs = k, b, vf
        else:
            ks, bs, vs = pltpu.roll(k, j, 0), pltpu.roll(b, j, 0), pltpu.roll(vf, j, 0)
        s = jnp.sum(q * ks * jnp.exp(b - bs), axis=-1, keepdims=True)
        o = o + jnp.where(tmod >= j, s, 0.0) * vs

    out = _rms(o, gn) * (zog * _sigmoid(zog))
    return out.astype(BF16), st_new


def _hgrn_kernel(q_ref, f_ref, i_ref, og_ref, lb_ref, gn_ref, o_ref, st_ref):
    @pl.when(pl.program_id(2) == 0)
    def _():
        st_ref[...] = jnp.zeros_like(st_ref)

    lbp = lb_ref[...]
    gn = gn_ref[...]

    def body(i, carry):
        rows = pl.ds(pl.multiple_of(i * HGRN_CHUNK, HGRN_CHUNK), HGRN_CHUNK)
        out, st_new = _hgrn_chunk(q_ref[rows, :].astype(F32), f_ref[rows, :].astype(F32), i_ref[rows, :],
                                  og_ref[rows, :].astype(F32), lbp, gn, st_ref[...])
        o_ref[rows, :] = out
        st_ref[...] = st_new
        return carry

    lax.fori_loop(0, q_ref.shape[0] // HGRN_CHUNK, body, 0)


def _hgrn(planes, lb_params, gn, *, first_plane):
    _, b, s, _ = planes.shape
    rows = min(HGRN_ROWS, s)
    h = HGRN_HEADS

    def plane_spec(k):
        return pl.BlockSpec((None, None, rows, LANES), lambda bi, hi, ci: (first_plane + k * h + hi, bi, ci, 0))

    return pl.pallas_call(
        _hgrn_kernel,
        grid=(b, h, s // rows),
        in_specs=[plane_spec(0), plane_spec(1), plane_spec(2), plane_spec(3),
                  pl.BlockSpec((None, 3, LANES), lambda bi, hi, ci: (hi, 0, 0)),
                  pl.BlockSpec((1, LANES), lambda bi, hi, ci: (0, 0))],
        out_specs=pl.BlockSpec((None, None, rows, LANES), lambda bi, hi, ci: (hi, bi, ci, 0)),
        out_shape=jax.ShapeDtypeStruct((h, b, s, LANES), BF16),
        scratch_shapes=[pltpu.VMEM((HGRN_HEAD_DIM, HGRN_HEAD_DIM), F32)],
        compiler_params=_cparams(("parallel", "parallel", "arbitrary")),
        name="hgrn2",
    )(planes, planes, planes, planes, lb_params, gn.reshape(1, LANES))


def _sb_kernel(q_ref, k_ref, v_ref, o_ref):
    tq = q_ref.shape[0]
    tk = SB_BLOCK
    qi = pl.program_id(2)
    q2 = q_ref[...] * jnp.asarray(SB_HEAD_DIM ** -0.5, BF16)
    lane = lax.broadcasted_iota(jnp.int32, (1, LANES), 1)
    first = lane < SB_HEAD_DIM
    q_heads = (jnp.where(first, q2, jnp.zeros_like(q2)), jnp.where(first, jnp.zeros_like(q2), q2))
    row = lax.broadcasted_iota(jnp.int32, (tq, tk), 0)
    col = lax.broadcasted_iota(jnp.int32, (tq, tk), 1)
    later = jnp.where(lax.broadcasted_iota(jnp.int32, (tk, tk), 0) > lax.broadcasted_iota(jnp.int32, (tk, tk), 1),
                      1.0, 0.0).astype(BF16)

    def body(it, carry):
        kb = qi - it
        accs, sums = carry
        rows = pl.ds(pl.multiple_of(kb * tk, tk), tk)
        k2 = k_ref[rows, :]
        v2 = v_ref[rows, :]
        strict = (col - it * tk) < row
        new_accs, new_sums = [], []
        for h in range(SB_HEADS_PER_PLANE):
            z = _dot_nt(q_heads[h], k2)
            softplus = jnp.maximum(z, 0.0) + jnp.log(1.0 + jnp.exp(-jnp.abs(z)))
            log_keep = jnp.where(strict, -softplus, 0.0)
            after = _dot(log_keep.astype(BF16), later)
            w = jnp.where(strict, jnp.exp(z + log_keep + after + sums[h]), 0.0)
            new_accs.append(accs[h] + _dot(w.astype(BF16), v2))
            new_sums.append(sums[h] + jnp.sum(log_keep, axis=-1, keepdims=True))
        return tuple(new_accs), tuple(new_sums)

    zero_acc = jnp.zeros((tq, LANES), F32)
    zero_sum = jnp.zeros((tq, 1), F32)
    accs, _ = lax.fori_loop(0, qi + 1, body, ((zero_acc, zero_acc), (zero_sum, zero_sum)))
    o_ref[...] = jnp.where(first, accs[0], accs[1]).astype(BF16)


def _sb_attention(planes, *, first_plane):
    _, b, s, _ = planes.shape
    hp = SB_HEADS // SB_HEADS_PER_PLANE
    tq = SB_BLOCK
    return pl.pallas_call(
        _sb_kernel,
        grid=(b, hp, s // tq),
        in_specs=[pl.BlockSpec((None, None, tq, LANES), lambda bi, hi, qi: (first_plane + hi, bi, qi, 0)),
                  pl.BlockSpec((None, None, s, LANES), lambda bi, hi, qi: (first_plane + hp + hi, bi, 0, 0)),
                  pl.BlockSpec((None, None, s, LANES), lambda bi, hi, qi: (first_plane + 2 * hp + hi, bi, 0, 0))],
        out_specs=pl.BlockSpec((None, None, tq, LANES), lambda bi, hi, qi: (hi, bi, qi, 0)),
        out_shape=jax.ShapeDtypeStruct((hp, b, s, LANES), BF16),
        compiler_params=_cparams(("parallel", "parallel", "arbitrary")),
        name="stickbreak",
    )(planes, planes, planes)


def _route(h, wr):
    logits = jnp.dot(h, wr, preferred_element_type=F32, precision=lax.Precision.HIGHEST)
    lane = lax.broadcasted_iota(jnp.int32, (1, LANES), 1).astype(F32)
    neg = jnp.float32(-jnp.inf)
    lg = jnp.where(lane < N_EXPERTS, logits, neg)
    m1 = jnp.max(lg, axis=-1, keepdims=True)
    i1 = jnp.min(jnp.where(lg == m1, lane, float(LANES)), axis=-1, keepdims=True)
    sel1 = lane == i1
    lg2 = jnp.where(sel1, neg, lg)
    m2 = jnp.max(lg2, axis=-1, keepdims=True)
    i2 = jnp.min(jnp.where(lg2 == m2, lane, float(LANES)), axis=-1, keepdims=True)
    sel2 = lane == i2
    e2 = jnp.exp(m2 - m1)
    w1 = 1.0 / (1.0 + e2)
    return jnp.where(sel1, w1, 0.0) + jnp.where(sel2, e2 * w1, 0.0)


def _combine_kernel(*refs, with_router):
    if with_router:
        x_ref, oa_ref, ob_ref, ga_ref, gb_ref, pa_ref, pb_ref, wo_ref, g2_ref, wr_ref, xo_ref, h2_ref, cw_ref = refs
    else:
        x_ref, oa_ref, ob_ref, ga_ref, gb_ref, pa_ref, pb_ref, wo_ref, g2_ref, xo_ref, h2_ref = refs

    def cat(ref):
        return jnp.concatenate([ref[c] for c in range(ref.shape[0])], axis=1)

    ya = _dot(cat(oa_ref), pa_ref[...])
    yb = _dot(cat(ob_ref), pb_ref[...])
    y = _sigmoid(cat(ga_ref).astype(F32)) * ya + _sigmoid(cat(gb_ref).astype(F32)) * yb
    xn = x_ref[...] + _dot(y.astype(BF16), wo_ref[...])
    xo_ref[...] = xn
    h2 = _rms(xn, g2_ref[...])
    h2_ref[...] = h2.astype(BF16)
    if with_router:
        cw_ref[...] = _route(h2, wr_ref[...])


def _combine(x2, oa, ob, planes, pa, pb, wo, g2, wr, *, gate_plane, tm=512):
    t, d = x2.shape
    tm = min(tm, t)
    na, nb = oa.shape[0], ob.shape[0]
    ng = d // LANES
    with_router = wr is not None
    gblk = gate_plane // ng
    in_specs = [pl.BlockSpec((tm, d), lambda i: (i, 0)),
                pl.BlockSpec((na, tm, LANES), lambda i: (0, i, 0)),
                pl.BlockSpec((nb, tm, LANES), lambda i: (0, i, 0)),
                pl.BlockSpec((ng, tm, LANES), lambda i: (gblk, i, 0)),
                pl.BlockSpec((ng, tm, LANES), lambda i: (gblk + 1, i, 0)),
                pl.BlockSpec(pa.shape, lambda i: (0, 0)),
                pl.BlockSpec(pb.shape, lambda i: (0, 0)),
                pl.BlockSpec(wo.shape, lambda i: (0, 0)),
                pl.BlockSpec((1, d), lambda i: (0, 0))]
    args = [x2, oa, ob, planes, planes, pa, pb, wo, g2.reshape(1, d)]
    out_specs = [pl.BlockSpec((tm, d), lambda i: (i, 0)), pl.BlockSpec((tm, d), lambda i: (i, 0))]
    out_shape = [jax.ShapeDtypeStruct((t, d), F32), jax.ShapeDtypeStruct((t, d), BF16)]
    if with_router:
        in_specs.append(pl.BlockSpec(wr.shape, lambda i: (0, 0)))
        args.append(wr)
        out_specs.append(pl.BlockSpec((tm, LANES), lambda i: (i, 0)))
        out_shape.append(jax.ShapeDtypeStruct((t, LANES), F32))
    return pl.pallas_call(
        functools.partial(_combine_kernel, with_router=with_router),
        grid=(t // tm,),
        in_specs=in_specs, out_specs=out_specs, out_shape=out_shape,
        compiler_params=_cparams(("parallel",)),
        name="combine_router" if with_router else "combine",
    )(*args)


def _swiglu_kernel(*refs, weighted, final_norm):
    refs = list(refs)
    h_ref, x_ref = refs[0], refs[1]
    pos = 2
    cw_ref = None
    if weighted:
        cw_ref = refs[pos]
        pos += 1
    wg_ref, wu_ref, wd_ref = refs[pos:pos + 3]
    pos += 3
    gf_ref = None
    if final_norm:
        gf_ref = refs[pos]
        pos += 1
    o_ref, acc_ref = refs[pos], refs[pos + 1]

    e = pl.program_id(1)
    j = pl.program_id(2)
    first = jnp.logical_and(e == 0, j == 0)
    last = jnp.logical_and(e == pl.num_programs(1) - 1, j == pl.num_programs(2) - 1)

    @pl.when(first)
    def _():
        acc_ref[...] = jnp.zeros_like(acc_ref)

    h = h_ref[...]
    a = _dot(h, wg_ref[...])
    u = _dot(h, wu_ref[...])
    g = a * _sigmoid(a) * u
    if weighted:
        lane = lax.broadcasted_iota(jnp.int32, (1, LANES), 1)
        g = g * jnp.sum(jnp.where(lane == e, cw_ref[...], 0.0), axis=-1, keepdims=True)
    acc_ref[...] += _dot(g.astype(BF16), wd_ref[...])

    @pl.when(last)
    def _():
        y = x_ref[...] + acc_ref[...]
        if final_norm:
            y = _rms(y, gf_ref[...])
        o_ref[...] = y


def _swiglu(h2, x2, cw, wg, wu, wd, gf, *, tm=1024, tf=512):
    t, d = x2.shape
    ne, _, f = wg.shape
    tm = min(tm, t)
    weighted = cw is not None
    final_norm = gf is not None
    in_specs = [pl.BlockSpec((tm, d), lambda i, e, j: (i, 0)),
                pl.BlockSpec((tm, d), lambda i, e, j: (i, 0))]
    args = [h2, x2]
    if weighted:
        in_specs.append(pl.BlockSpec((tm, LANES), lambda i, e, j: (i, 0)))
        args.append(cw)
    in_specs += [pl.BlockSpec((None, d, tf), lambda i, e, j: (e, 0, j)),
                 pl.BlockSpec((None, d, tf), lambda i, e, j: (e, 0, j)),
                 pl.BlockSpec((None, tf, d), lambda i, e, j: (e, j, 0))]
    args += [wg, wu, wd]
    if final_norm:
        in_specs.append(pl.BlockSpec((1, d), lambda i, e, j: (0, 0)))
        args.append(gf.reshape(1, d))
    return pl.pallas_call(
        functools.partial(_swiglu_kernel, weighted=weighted, final_norm=final_norm),
        grid=(t // tm, ne, f // tf),
        in_specs=in_specs,
        out_specs=pl.BlockSpec((tm, d), lambda i, e, j: (i, 0)),
        out_shape=jax.ShapeDtypeStruct((t, d), F32),
        scratch_shapes=[pltpu.VMEM((tm, d), F32)],
        compiler_params=_cparams(("parallel", "arbitrary", "arbitrary")),
        name="swiglu_moe" if weighted else "swiglu",
    )(*args)


def kernel(x, mix_norm, w_in, hgrn_lb_logits, hgrn_out_norm, w_branch_hgrn, w_branch_sb, w_out, ffn_norm,
           dense_w_gate, dense_w_up, dense_w_down, moe_router, moe_w_gate, moe_w_up, moe_w_down, final_norm):
    b, s, d = x.shape
    t = b * s
    depth = w_in.shape[0]
    hgrn_width = HGRN_HEADS * HGRN_HEAD_DIM
    sb_width = SB_HEADS * SB_HEAD_DIM
    gate_cols = 4 * hgrn_width + 3 * sb_width
    n_gate_planes = 2 * d // LANES
    hgrn_plane = n_gate_planes
    sb_plane = hgrn_plane + 4 * hgrn_width // LANES

    lb_all = jnp.cumsum(jax.nn.softmax(hgrn_lb_logits.astype(F32), axis=0), axis=0)
    lb_all = lb_all - lb_all[0:1]
    lb_params = jnp.stack([jnp.log(lb_all), jnp.log1p(-lb_all), 1.0 - lb_all], axis=1)
    lb_params = lb_params.reshape(depth, 3, HGRN_HEADS, HGRN_HEAD_DIM).transpose(0, 2, 1, 3)

    x2 = x.reshape(t, d)
    for layer in range(depth):
        w = w_in[layer]
        w = jnp.concatenate([w[:, gate_cols:], w[:, :gate_cols]], axis=1).astype(BF16)
        planes = _inproj(x2, mix_norm[layer], w)
        planes4 = planes.reshape(planes.shape[0], b, s, LANES)
        oa = _hgrn(planes4, lb_params[layer], hgrn_out_norm[layer], first_plane=hgrn_plane)
        ob = _sb_attention(planes4, first_plane=sb_plane)
        j = layer // 2
        moe = layer % 2 == 1
        wr = None
        if moe:
            wr = jnp.zeros((d, LANES), F32).at[:, :N_EXPERTS].set(moe_router[j].astype(F32))
        res = _combine(x2, oa.reshape(HGRN_HEADS, t, LANES), ob.reshape(ob.shape[0], t, LANES), planes,
                       w_branch_hgrn[layer].astype(BF16), w_branch_sb[layer].astype(BF16),
                       w_out[layer].astype(BF16), ffn_norm[layer], wr, gate_plane=0)
        gf = final_norm if layer == depth - 1 else None
        if moe:
            xn, h2, cw = res
            x2 = _swiglu(h2, xn, cw, moe_w_gate[j].astype(BF16), moe_w_up[j].astype(BF16),
                         moe_w_down[j].astype(BF16), gf)
        else:
            xn, h2 = res
            x2 = _swiglu(h2, xn, None, dense_w_gate[j][None].astype(BF16), dense_w_up[j][None].astype(BF16),
                         dense_w_down[j][None].astype(BF16), gf)
    return x2.reshape(b, s, d)
```

```python
import functools

import jax
import jax.numpy as jnp
from jax import lax
from jax.experimental import pallas as pl
from jax.experimental.pallas import tpu as pltpu

F32 = jnp.float32
BF16 = jnp.bfloat16

LANES = 128
HGRN_HEADS = 4
HGRN_HEAD_DIM = 128
SB_HEADS = 8
SB_HEAD_DIM = 64
SB_HEADS_PER_PLANE = LANES // SB_HEAD_DIM
N_EXPERTS = 8
RMS_EPS = 1e-6
VMEM_LIMIT_BYTES = 56 * 1024 * 1024

HGRN_CHUNK = 128
HGRN_DIAG = 16
HGRN_ROWS = 512
SB_BLOCK = 256
SB_UNDERFLOW_LOG = -105.0


def _cparams(semantics):
    return pltpu.CompilerParams(dimension_semantics=semantics, vmem_limit_bytes=VMEM_LIMIT_BYTES)


def _rms(x, g):
    ms = jnp.mean(x * x, axis=-1, keepdims=True)
    return x * lax.rsqrt(ms + RMS_EPS) * g


def _sigmoid(x):
    return 1.0 / (1.0 + jnp.exp(-x))


def _dot(a, b):
    return jnp.dot(a, b, preferred_element_type=F32)


def _dot_nt(a, b):
    return lax.dot_general(a, b, (((1,), (1,)), ((), ())), preferred_element_type=F32)


def _dot_tn(a, b):
    return lax.dot_general(a, b, (((0,), (0,)), ((), ())), preferred_element_type=F32)


def _inproj_kernel(x_ref, g_ref, w_ref, o_ref, *, tn):
    h = _rms(x_ref[...], g_ref[...]).astype(BF16)
    for j in range(w_ref.shape[1] // tn):
        r = _dot(h, w_ref[:, j * tn:(j + 1) * tn])
        for c in range(tn // LANES):
            o_ref[j * (tn // LANES) + c] = r[:, c * LANES:(c + 1) * LANES].astype(BF16)


def _inproj(x2, g, w, *, tm=512, tn=512):
    t, d = x2.shape
    n = w.shape[1]
    tm = min(tm, t)
    return pl.pallas_call(
        functools.partial(_inproj_kernel, tn=tn),
        grid=(t // tm,),
        in_specs=[pl.BlockSpec((tm, d), lambda i: (i, 0)),
                  pl.BlockSpec((1, d), lambda i: (0, 0)),
                  pl.BlockSpec((d, n), lambda i: (0, 0))],
        out_specs=pl.BlockSpec((n // LANES, tm, LANES), lambda i: (0, i, 0)),
        out_shape=jax.ShapeDtypeStruct((n // LANES, t, LANES), BF16),
        compiler_params=_cparams(("parallel",)),
        name="inproj",
    )(x2, g.reshape(1, d), w)


def _hgrn_chunk(zq, zf, vi, zog, lbp, gn, st):
    c = zq.shape[0]
    log_lb, log1m_lb, one_m_lb = lbp[0:1], lbp[1:2], lbp[2:3]

    q = zq * _sigmoid(zq)
    e = jnp.exp(-jnp.abs(zf))
    den = 1.0 + e
    log_sig = jnp.minimum(zf, 0.0) - jnp.log(den)
    k = one_m_lb * (jnp.where(zf >= 0.0, e, 1.0) / den)
    cc = log1m_lb + log_sig
    log_f = jnp.maximum(log_lb, cc) + jnp.log(1.0 + jnp.exp(-jnp.abs(log_lb - cc)))

    hi = log_f.astype(BF16)
    r1 = log_f - hi.astype(F32)
    mid = r1.astype(BF16)
    lo = (r1 - mid.astype(F32)).astype(BF16)
    row = lax.broadcasted_iota(jnp.int32, (c, c), 0)
    col = lax.broadcasted_iota(jnp.int32, (c, c), 1)
    tri = jnp.where(col <= row, 1.0, 0.0).astype(BF16)
    b3 = _dot(tri, jnp.concatenate([hi, mid, lo], axis=1))
    b = b3[:, :LANES] + b3[:, LANES:2 * LANES] + b3[:, 2 * LANES:]
    b_prev = b - log_f
    b_last = b[c - 1:c, :]

    o = _dot_nt((q * jnp.exp(b)).astype(BF16), st.astype(BF16))
    k_tail = (k * jnp.exp(b_last - b)).astype(BF16)
    st_new = st * jnp.exp(b_last) + _dot_tn(vi, k_tail)

    rows = lax.broadcasted_iota(jnp.int32, (c, 1), 0)
    scores = jnp.zeros((c, c), F32)
    m = HGRN_DIAG
    while m < c:
        n = c // m
        base = b_prev.reshape(n, m, LANES)[:, 0:1, :]
        p = b.reshape(n, m, LANES) - base
        suf = (p[:, m - 1:m, :] - p).reshape(c, LANES)
        p = p.reshape(c, LANES)
        odd = ((rows // m) % 2) == 1
        q_m = jnp.where(odd, q * jnp.exp(p), 0.0).astype(BF16)
        k_m = jnp.where(odd, 0.0, k * jnp.exp(suf)).astype(BF16)
        s_m = _dot_nt(q_m, k_m)
        scores = scores + jnp.where((row // (2 * m)) == (col // (2 * m)), s_m, 0.0)
        m *= 2
    o = o + _dot(scores.astype(BF16), vi)

    vf = vi.astype(F32)
    tmod = rows % HGRN_DIAG
    for j in range(HGRN_DIAG):
        if j == 0:
            ks, bs, vs = k, b, vf
        else:
            ks, bs, vs = pltpu.roll(k, j, 0), pltpu.roll(b, j, 0), pltpu.roll(vf, j, 0)
        s = jnp.sum(q * ks * jnp.exp(b - bs), axis=-1, keepdims=True)
        o = o + jnp.where(tmod >= j, s, 0.0) * vs

    out = _rms(o, gn) * (zog * _sigmoid(zog))
    return out.astype(BF16), st_new


def _hgrn_kernel(q_ref, f_ref, i_ref, og_ref, lb_ref, gn_ref, o_ref, st_ref):
    @pl.when(pl.program_id(2) == 0)
    def _():
        st_ref[...] = jnp.zeros_like(st_ref)

    lbp = lb_ref[...]
    gn = gn_ref[...]

    def body(i, carry):
        rows = pl.ds(pl.multiple_of(i * HGRN_CHUNK, HGRN_CHUNK), HGRN_CHUNK)
        out, st_new = _hgrn_chunk(q_ref[rows, :].astype(F32), f_ref[rows, :].astype(F32), i_ref[rows, :],
                                  og_ref[rows, :].astype(F32), lbp, gn, st_ref[...])
        o_ref[rows, :] = out
        st_ref[...] = st_new
        return carry

    lax.fori_loop(0, q_ref.shape[0] // HGRN_CHUNK, body, 0)


def _hgrn(planes, lb_params, gn, *, first_plane):
    _, b, s, _ = planes.shape
    rows = min(HGRN_ROWS, s)
    h = HGRN_HEADS

    def plane_spec(k):
        return pl.BlockSpec((None, None, rows, LANES), lambda bi, hi, ci: (first_plane + k * h + hi, bi, ci, 0))

    return pl.pallas_call(
        _hgrn_kernel,
        grid=(b, h, s // rows),
        in_specs=[plane_spec(0), plane_spec(1), plane_spec(2), plane_spec(3),
                  pl.BlockSpec((None, 3, LANES), lambda bi, hi, ci: (hi, 0, 0)),
                  pl.BlockSpec((1, LANES), lambda bi, hi, ci: (0, 0))],
        out_specs=pl.BlockSpec((None, None, rows, LANES), lambda bi, hi, ci: (hi, bi, ci, 0)),
        out_shape=jax.ShapeDtypeStruct((h, b, s, LANES), BF16),
        scratch_shapes=[pltpu.VMEM((HGRN_HEAD_DIM, HGRN_HEAD_DIM), F32)],
        compiler_params=_cparams(("parallel", "parallel", "arbitrary")),
        name="hgrn2",
    )(planes, planes, planes, planes, lb_params, gn.reshape(1, LANES))


def _sb_kernel(q_ref, k_ref, v_ref, o_ref):
    tq = q_ref.shape[0]
    tk = tq
    qi = pl.program_id(2)
    q2 = q_ref[...] * jnp.asarray(SB_HEAD_DIM ** -0.5, BF16)
    lane = lax.broadcasted_iota(jnp.int32, (1, LANES), 1)
    first = lane < SB_HEAD_DIM
    q_heads = (jnp.where(first, q2, jnp.zeros_like(q2)), jnp.where(first, jnp.zeros_like(q2), q2))
    strict = lax.broadcasted_iota(jnp.int32, (tq, tk), 1) < lax.broadcasted_iota(jnp.int32, (tq, tk), 0)
    later = jnp.where(lax.broadcasted_iota(jnp.int32, (tk, tk), 0) > lax.broadcasted_iota(jnp.int32, (tk, tk), 1),
                      1.0, 0.0).astype(BF16)

    def block(kb, accs, sums, diagonal):
        rows = pl.ds(pl.multiple_of(kb * tk, tk), tk)
        k2 = k_ref[rows, :]
        v2 = v_ref[rows, :]
        new_accs, new_sums = [], []
        for h in range(SB_HEADS_PER_PLANE):
            z = _dot_nt(q_heads[h], k2)
            log_keep = -(jnp.maximum(z, 0.0) + jnp.log(1.0 + jnp.exp(-jnp.abs(z))))
            if diagonal:
                log_keep = jnp.where(strict, log_keep, 0.0)
            after = _dot(log_keep.astype(BF16), later)
            w = jnp.exp(z + log_keep + after + sums[h])
            if diagonal:
                w = jnp.where(strict, w, 0.0)
            new_accs.append(accs[h] + _dot(w.astype(BF16), v2))
            new_sums.append(sums[h] + jnp.sum(log_keep, axis=-1, keepdims=True))
        top = jnp.max(jnp.maximum(new_sums[0], new_sums[1]))
        return tuple(new_accs), tuple(new_sums), (top < SB_UNDERFLOW_LOG).astype(jnp.int32)

    zero_acc = jnp.zeros((tq, LANES), F32)
    zero_sum = jnp.zeros((tq, 1), F32)
    accs, sums, done = block(qi, (zero_acc, zero_acc), (zero_sum, zero_sum), True)

    def cond(carry):
        it, done, _, _ = carry
        return jnp.logical_and(it <= qi, done == 0)

    def body(carry):
        it, _, accs, sums = carry
        accs, sums, done = block(qi - it, accs, sums, False)
        return it + 1, done, accs, sums

    _, _, accs, _ = lax.while_loop(cond, body, (jnp.int32(1), done, accs, sums))
    o_ref[...] = jnp.where(first, accs[0], accs[1]).astype(BF16)


def _sb_attention(planes, *, first_plane):
    _, b, s, _ = planes.shape
    hp = SB_HEADS // SB_HEADS_PER_PLANE
    tq = SB_BLOCK
    return pl.pallas_call(
        _sb_kernel,
        grid=(b, hp, s // tq),
        in_specs=[pl.BlockSpec((None, None, tq, LANES), lambda bi, hi, qi: (first_plane + hi, bi, qi, 0)),
                  pl.BlockSpec((None, None, s, LANES), lambda bi, hi, qi: (first_plane + hp + hi, bi, 0, 0)),
                  pl.BlockSpec((None, None, s, LANES), lambda bi, hi, qi: (first_plane + 2 * hp + hi, bi, 0, 0))],
        out_specs=pl.BlockSpec((None, None, tq, LANES), lambda bi, hi, qi: (hi, bi, qi, 0)),
        out_shape=jax.ShapeDtypeStruct((hp, b, s, LANES), BF16),
        compiler_params=_cparams(("parallel", "parallel", "arbitrary")),
        name="stickbreak",
    )(planes, planes, planes)


def _route(h, wr):
    logits = jnp.dot(h, wr, preferred_element_type=F32, precision=lax.Precision.HIGHEST)
    lane = lax.broadcasted_iota(jnp.int32, (1, LANES), 1).astype(F32)
    neg = jnp.float32(-jnp.inf)
    lg = jnp.where(lane < N_EXPERTS, logits, neg)
    m1 = jnp.max(lg, axis=-1, keepdims=True)
    i1 = jnp.min(jnp.where(lg == m1, lane, float(LANES)), axis=-1, keepdims=True)
    sel1 = lane == i1
    lg2 = jnp.where(sel1, neg, lg)
    m2 = jnp.max(lg2, axis=-1, keepdims=True)
    i2 = jnp.min(jnp.where(lg2 == m2, lane, float(LANES)), axis=-1, keepdims=True)
    sel2 = lane == i2
    e2 = jnp.exp(m2 - m1)
    w1 = 1.0 / (1.0 + e2)
    return jnp.where(sel1, w1, 0.0) + jnp.where(sel2, e2 * w1, 0.0)


def _combine_kernel(*refs, with_router):
    if with_router:
        x_ref, oa_ref, ob_ref, ga_ref, gb_ref, pa_ref, pb_ref, wo_ref, g2_ref, wr_ref, xo_ref, h2_ref, cw_ref = refs
    else:
        x_ref, oa_ref, ob_ref, ga_ref, gb_ref, pa_ref, pb_ref, wo_ref, g2_ref, xo_ref, h2_ref = refs

    def cat(ref):
        return jnp.concatenate([ref[c] for c in range(ref.shape[0])], axis=1)

    ya = _dot(cat(oa_ref), pa_ref[...])
    yb = _dot(cat(ob_ref), pb_ref[...])
    y = _sigmoid(cat(ga_ref).astype(F32)) * ya + _sigmoid(cat(gb_ref).astype(F32)) * yb
    xn = x_ref[...] + _dot(y.astype(BF16), wo_ref[...])
    xo_ref[...] = xn
    h2 = _rms(xn, g2_ref[...])
    h2_ref[...] = h2.astype(BF16)
    if with_router:
        cw_ref[...] = _route(h2, wr_ref[...])


def _combine(x2, oa, ob, planes, pa, pb, wo, g2, wr, *, gate_plane, tm=512):
    t, d = x2.shape
    tm = min(tm, t)
    na, nb = oa.shape[0], ob.shape[0]
    ng = d // LANES
    with_router = wr is not None
    gblk = gate_plane // ng
    in_specs = [pl.BlockSpec((tm, d), lambda i: (i, 0)),
                pl.BlockSpec((na, tm, LANES), lambda i: (0, i, 0)),
                pl.BlockSpec((nb, tm, LANES), lambda i: (0, i, 0)),
                pl.BlockSpec((ng, tm, LANES), lambda i: (gblk, i, 0)),
                pl.BlockSpec((ng, tm, LANES), lambda i: (gblk + 1, i, 0)),
                pl.BlockSpec(pa.shape, lambda i: (0, 0)),
                pl.BlockSpec(pb.shape, lambda i: (0, 0)),
                pl.BlockSpec(wo.shape, lambda i: (0, 0)),
                pl.BlockSpec((1, d), lambda i: (0, 0))]
    args = [x2, oa, ob, planes, planes, pa, pb, wo, g2.reshape(1, d)]
    out_specs = [pl.BlockSpec((tm, d), lambda i: (i, 0)), pl.BlockSpec((tm, d), lambda i: (i, 0))]
    out_shape = [jax.ShapeDtypeStruct((t, d), F32), jax.ShapeDtypeStruct((t, d), BF16)]
    if with_router:
        in_specs.append(pl.BlockSpec(wr.shape, lambda i: (0, 0)))
        args.append(wr)
        out_specs.append(pl.BlockSpec((tm, LANES), lambda i: (i, 0)))
        out_shape.append(jax.ShapeDtypeStruct((t, LANES), F32))
    return pl.pallas_call(
        functools.partial(_combine_kernel, with_router=with_router),
        grid=(t // tm,),
        in_specs=in_specs, out_specs=out_specs, out_shape=out_shape,
        compiler_params=_cparams(("parallel",)),
        name="combine_router" if with_router else "combine",
    )(*args)


def _swiglu_kernel(*refs, weighted, final_norm):
    refs = list(refs)
    h_ref, x_ref = refs[0], refs[1]
    pos = 2
    cw_ref = None
    if weighted:
        cw_ref = refs[pos]
        pos += 1
    wg_ref, wu_ref, wd_ref = refs[pos:pos + 3]
    pos += 3
    gf_ref = None
    if final_norm:
        gf_ref = refs[pos]
        pos += 1
    o_ref, acc_ref = refs[pos], refs[pos + 1]

    e = pl.program_id(1)
    j = pl.program_id(2)
    first = jnp.logical_and(e == 0, j == 0)
    last = jnp.logical_and(e == pl.num_programs(1) - 1, j == pl.num_programs(2) - 1)

    @pl.when(first)
    def _():
        acc_ref[...] = jnp.zeros_like(acc_ref)

    h = h_ref[...]
    a = _dot(h, wg_ref[...])
    u = _dot(h, wu_ref[...])
    g = a * _sigmoid(a) * u
    if weighted:
        lane = lax.broadcasted_iota(jnp.int32, (1, LANES), 1)
        g = g * jnp.sum(jnp.where(lane == e, cw_ref[...], 0.0), axis=-1, keepdims=True)
    acc_ref[...] += _dot(g.astype(BF16), wd_ref[...])

    @pl.when(last)
    def _():
        y = x_ref[...] + acc_ref[...]
        if final_norm:
            y = _rms(y, gf_ref[...])
        o_ref[...] = y


def _swiglu(h2, x2, cw, wg, wu, wd, gf, *, tm=1024, tf=512):
    t, d = x2.shape
    ne, _, f = wg.shape
    tm = min(tm, t)
    weighted = cw is not None
    final_norm = gf is not None
    in_specs = [pl.BlockSpec((tm, d), lambda i, e, j: (i, 0)),
                pl.BlockSpec((tm, d), lambda i, e, j: (i, 0))]
    args = [h2, x2]
    if weighted:
        in_specs.append(pl.BlockSpec((tm, LANES), lambda i, e, j: (i, 0)))
        args.append(cw)
    in_specs += [pl.BlockSpec((None, d, tf), lambda i, e, j: (e, 0, j)),
                 pl.BlockSpec((None, d, tf), lambda i, e, j: (e, 0, j)),
                 pl.BlockSpec((None, tf, d), lambda i, e, j: (e, j, 0))]
    args += [wg, wu, wd]
    if final_norm:
        in_specs.append(pl.BlockSpec((1, d), lambda i, e, j: (0, 0)))
        args.append(gf.reshape(1, d))
    return pl.pallas_call(
        functools.partial(_swiglu_kernel, weighted=weighted, final_norm=final_norm),
        grid=(t // tm, ne, f // tf),
        in_specs=in_specs,
        out_specs=pl.BlockSpec((tm, d), lambda i, e, j: (i, 0)),
        out_shape=jax.ShapeDtypeStruct((t, d), F32),
        scratch_shapes=[pltpu.VMEM((tm, d), F32)],
        compiler_params=_cparams(("parallel", "arbitrary", "arbitrary")),
        name="swiglu_moe" if weighted else "swiglu",
    )(*args)


def kernel(x, mix_norm, w_in, hgrn_lb_logits, hgrn_out_norm, w_branch_hgrn, w_branch_sb, w_out, ffn_norm,
           dense_w_gate, dense_w_up, dense_w_down, moe_router, moe_w_gate, moe_w_up, moe_w_down, final_norm):
    b, s, d = x.shape
    t = b * s
    depth = w_in.shape[0]
    hgrn_width = HGRN_HEADS * HGRN_HEAD_DIM
    sb_width = SB_HEADS * SB_HEAD_DIM
    gate_cols = 4 * hgrn_width + 3 * sb_width
    n_gate_planes = 2 * d // LANES
    hgrn_plane = n_gate_planes
    sb_plane = hgrn_plane + 4 * hgrn_width // LANES

    lb_all = jnp.cumsum(jax.nn.softmax(hgrn_lb_logits.astype(F32), axis=0), axis=0)
    lb_all = lb_all - lb_all[0:1]
    lb_params = jnp.stack([jnp.log(lb_all), jnp.log1p(-lb_all), 1.0 - lb_all], axis=1)
    lb_params = lb_params.reshape(depth, 3, HGRN_HEADS, HGRN_HEAD_DIM).transpose(0, 2, 1, 3)

    x2 = x.reshape(t, d)
    for layer in range(depth):
        w = w_in[layer]
        w = jnp.concatenate([w[:, gate_cols:], w[:, :gate_cols]], axis=1).astype(BF16)
        planes = _inproj(x2, mix_norm[layer], w)
        planes4 = planes.reshape(planes.shape[0], b, s, LANES)
        oa = _hgrn(planes4, lb_params[layer], hgrn_out_norm[layer], first_plane=hgrn_plane)
        ob = _sb_attention(planes4, first_plane=sb_plane)
        j = layer // 2
        moe = layer % 2 == 1
        wr = None
        if moe:
            wr = jnp.zeros((d, LANES), F32).at[:, :N_EXPERTS].set(moe_router[j].astype(F32))
        res = _combine(x2, oa.reshape(HGRN_HEADS, t, LANES), ob.reshape(ob.shape[0], t, LANES), planes,
                       w_branch_hgrn[layer].astype(BF16), w_branch_sb[layer].astype(BF16),
                       w_out[layer].astype(BF16), ffn_norm[layer], wr, gate_plane=0)
        gf = final_norm if layer == depth - 1 else None
        if moe:
            xn, h2, cw = res
            x2 = _swiglu(h2, xn, cw, moe_w_gate[j].astype(BF16), moe_w_up[j].astype(BF16),
                         moe_w_down[j].astype(BF16), gf)
        else:
            xn, h2 = res
            x2 = _swiglu(h2, xn, None, dense_w_gate[j][None].astype(BF16), dense_w_up[j][None].astype(BF16),
                         dense_w_down[j][None].astype(BF16), gf)
    return x2.reshape(b, s, d)
```

```python
import functools

import jax
import jax.numpy as jnp
from jax import lax
from jax.experimental import pallas as pl
from jax.experimental.pallas import tpu as pltpu
from jax.experimental.pallas import tpu_sc as plsc

F32 = jnp.float32
BF16 = jnp.bfloat16

LANES = 128
SUBLANES = 8
HGRN_HEADS = 4
HGRN_HEAD_DIM = 128
SB_HEADS = 8
SB_HEAD_DIM = 64
SB_HEADS_PER_PLANE = LANES // SB_HEAD_DIM
N_EXPERTS = 8
RMS_EPS = 1e-6
VMEM_LIMIT_BYTES = 56 * 1024 * 1024

HGRN_CHUNK = 128
HGRN_DIAG = 8
HGRN_ROWS = 512
SB_BLOCK = 256
MOE_TILE = 1024
SC_CHUNK = 64
SC_BUFFERS = 3
HIGH_HALF_MASK = -65536
SB_UNDERFLOW_LOG = -105.0


def _cparams(semantics):
    return pltpu.CompilerParams(dimension_semantics=semantics, vmem_limit_bytes=VMEM_LIMIT_BYTES)


def _rms(x, g):
    ms = jnp.mean(x * x, axis=-1, keepdims=True)
    return x * lax.rsqrt(ms + RMS_EPS) * g


def _sigmoid(x):
    return 1.0 / (1.0 + jnp.exp(-x))


def _dot(a, b):
    return jnp.dot(a, b, preferred_element_type=F32)


def _dot_nt(a, b):
    return lax.dot_general(a, b, (((1,), (1,)), ((), ())), preferred_element_type=F32)


def _dot_tn(a, b):
    return lax.dot_general(a, b, (((0,), (0,)), ((), ())), preferred_element_type=F32)


def _inproj_kernel(x_ref, g_ref, w_ref, o_ref, *, tn):
    h = _rms(x_ref[...], g_ref[...]).astype(BF16)
    for j in range(w_ref.shape[1] // tn):
        r = _dot(h, w_ref[:, j * tn:(j + 1) * tn])
        for c in range(tn // LANES):
            o_ref[j * (tn // LANES) + c] = r[:, c * LANES:(c + 1) * LANES].astype(BF16)


def _inproj(x2, g, w, *, tm=512, tn=512):
    t, d = x2.shape
    n = w.shape[1]
    tm = min(tm, t)
    return pl.pallas_call(
        functools.partial(_inproj_kernel, tn=tn),
        grid=(t // tm,),
        in_specs=[pl.BlockSpec((tm, d), lambda i: (i, 0)),
                  pl.BlockSpec((1, d), lambda i: (0, 0)),
                  pl.BlockSpec((d, n), lambda i: (0, 0))],
        out_specs=pl.BlockSpec((n // LANES, tm, LANES), lambda i: (0, i, 0)),
        out_shape=jax.ShapeDtypeStruct((n // LANES, t, LANES), BF16),
        compiler_params=_cparams(("parallel",)),
        name="inproj",
    )(x2, g.reshape(1, d), w)


def _hgrn_chunk(zq, zf, vi, zog, lbp, gn, st):
    c = zq.shape[0]
    log_lb, log1m_lb, one_m_lb = lbp[0:1], lbp[1:2], lbp[2:3]

    q = zq * _sigmoid(zq)
    e = jnp.exp(-jnp.abs(zf))
    den = 1.0 + e
    log_sig = jnp.minimum(zf, 0.0) - jnp.log(den)
    k = one_m_lb * (jnp.where(zf >= 0.0, e, 1.0) / den)
    cc = log1m_lb + log_sig
    log_f = jnp.maximum(log_lb, cc) + jnp.log(1.0 + jnp.exp(-jnp.abs(log_lb - cc)))

    hi = log_f.astype(BF16)
    r1 = log_f - hi.astype(F32)
    mid = r1.astype(BF16)
    lo = (r1 - mid.astype(F32)).astype(BF16)
    row = lax.broadcasted_iota(jnp.int32, (c, c), 0)
    col = lax.broadcasted_iota(jnp.int32, (c, c), 1)
    tri = jnp.where(col <= row, 1.0, 0.0).astype(BF16)
    b3 = _dot(tri, jnp.concatenate([hi, mid, lo], axis=1))
    b = b3[:, :LANES] + b3[:, LANES:2 * LANES] + b3[:, 2 * LANES:]
    b_prev = b - log_f
    b_last = b[c - 1:c, :]

    o = _dot_nt((q * jnp.exp(b)).astype(BF16), st.astype(BF16))
    k_tail = (k * jnp.exp(b_last - b)).astype(BF16)
    st_new = st * jnp.exp(b_last) + _dot_tn(vi, k_tail)

    rows = lax.broadcasted_iota(jnp.int32, (c, 1), 0)
    scores = jnp.zeros((c, c), F32)
    m = HGRN_DIAG
    while m < c:
        n = c // m
        base = b_prev.reshape(n, m, LANES)[:, 0:1, :]
        p = b.reshape(n, m, LANES) - base
        suf = (p[:, m - 1:m, :] - p).reshape(c, LANES)
        p = p.reshape(c, LANES)
        odd = ((rows // m) % 2) == 1
        q_m = jnp.where(odd, q * jnp.exp(p), 0.0).astype(BF16)
        k_m = jnp.where(odd, 0.0, k * jnp.exp(suf)).astype(BF16)
        s_m = _dot_nt(q_m, k_m)
        scores = scores + jnp.where((row // (2 * m)) == (col // (2 * m)), s_m, 0.0)
        m *= 2
    o = o + _dot(scores.astype(BF16), vi)

    vf = vi.astype(F32)
    tmod = rows % HGRN_DIAG
    for j in range(HGRN_DIAG):
        if j == 0:
            ks, bs, vs = k, b, vf
        else:
            ks, bs, vs = pltpu.roll(k, j, 0), pltpu.roll(b, j, 0), pltpu.roll(vf, j, 0)
        s = jnp.sum(q * ks * jnp.exp(b - bs), axis=-1, keepdims=True)
        o = o + jnp.where(tmod >= j, s, 0.0) * vs

    out = _rms(o, gn) * (zog * _sigmoid(zog))
    return out.astype(BF16), st_new


def _hgrn_kernel(q_ref, f_ref, i_ref, og_ref, lb_ref, gn_ref, o_ref, st_ref):
    @pl.when(pl.program_id(2) == 0)
    def _():
        st_ref[...] = jnp.zeros_like(st_ref)

    lbp = lb_ref[...]
    gn = gn_ref[...]

    def body(i, carry):
        rows = pl.ds(pl.multiple_of(i * HGRN_CHUNK, HGRN_CHUNK), HGRN_CHUNK)
        out, st_new = _hgrn_chunk(q_ref[rows, :].astype(F32), f_ref[rows, :].astype(F32), i_ref[rows, :],
                                  og_ref[rows, :].astype(F32), lbp, gn, st_ref[...])
        o_ref[rows, :] = out
        st_ref[...] = st_new
        return carry

    lax.fori_loop(0, q_ref.shape[0] // HGRN_CHUNK, body, 0)


def _hgrn(planes, lb_params, gn, *, first_plane):
    _, b, s, _ = planes.shape
    rows = min(HGRN_ROWS, s)
    h = HGRN_HEADS

    def plane_spec(k):
        return pl.BlockSpec((None, None, rows, LANES), lambda bi, hi, ci: (first_plane + k * h + hi, bi, ci, 0))

    return pl.pallas_call(
        _hgrn_kernel,
        grid=(b, h, s // rows),
        in_specs=[plane_spec(0), plane_spec(1), plane_spec(2), plane_spec(3),
                  pl.BlockSpec((None, 3, LANES), lambda bi, hi, ci: (hi, 0, 0)),
                  pl.BlockSpec((1, LANES), lambda bi, hi, ci: (0, 0))],
        out_specs=pl.BlockSpec((None, None, rows, LANES), lambda bi, hi, ci: (hi, bi, ci, 0)),
        out_shape=jax.ShapeDtypeStruct((h, b, s, LANES), BF16),
        scratch_shapes=[pltpu.VMEM((HGRN_HEAD_DIM, HGRN_HEAD_DIM), F32)],
        compiler_params=_cparams(("parallel", "parallel", "arbitrary")),
        name="hgrn2",
    )(planes, planes, planes, planes, lb_params, gn.reshape(1, LANES))


def _sb_kernel(q_ref, k_ref, v_ref, o_ref):
    tq = q_ref.shape[0]
    tk = tq
    qi = pl.program_id(2)
    q2 = q_ref[...] * jnp.asarray(SB_HEAD_DIM ** -0.5, BF16)
    lane = lax.broadcasted_iota(jnp.int32, (1, LANES), 1)
    first = lane < SB_HEAD_DIM
    q_heads = (jnp.where(first, q2, jnp.zeros_like(q2)), jnp.where(first, jnp.zeros_like(q2), q2))
    strict = lax.broadcasted_iota(jnp.int32, (tq, tk), 1) < lax.broadcasted_iota(jnp.int32, (tq, tk), 0)
    later = jnp.where(lax.broadcasted_iota(jnp.int32, (tk, tk), 0) > lax.broadcasted_iota(jnp.int32, (tk, tk), 1),
                      1.0, 0.0).astype(BF16)

    def block(kb, accs, sums, diagonal):
        rows = pl.ds(pl.multiple_of(kb * tk, tk), tk)
        k2 = k_ref[rows, :]
        v2 = v_ref[rows, :]
        new_accs, new_sums = [], []
        for h in range(SB_HEADS_PER_PLANE):
            z = _dot_nt(q_heads[h], k2)
            log_keep = -(jnp.maximum(z, 0.0) + jnp.log(1.0 + jnp.exp(-jnp.abs(z))))
            if diagonal:
                log_keep = jnp.where(strict, log_keep, 0.0)
            after = _dot(log_keep.astype(BF16), later)
            w = jnp.exp(z + log_keep + after + sums[h])
            if diagonal:
                w = jnp.where(strict, w, 0.0)
            new_accs.append(accs[h] + _dot(w.astype(BF16), v2))
            new_sums.append(sums[h] + jnp.sum(log_keep, axis=-1, keepdims=True))
        top = jnp.max(jnp.maximum(new_sums[0], new_sums[1]))
        return tuple(new_accs), tuple(new_sums), (top < SB_UNDERFLOW_LOG).astype(jnp.int32)

    zero_acc = jnp.zeros((tq, LANES), F32)
    zero_sum = jnp.zeros((tq, 1), F32)
    accs, sums, done = block(qi, (zero_acc, zero_acc), (zero_sum, zero_sum), True)

    def cond(carry):
        it, done, _, _ = carry
        return jnp.logical_and(it <= qi, done == 0)

    def body(carry):
        it, _, accs, sums = carry
        accs, sums, done = block(qi - it, accs, sums, False)
        return it + 1, done, accs, sums

    _, _, accs, _ = lax.while_loop(cond, body, (jnp.int32(1), done, accs, sums))
    o_ref[...] = jnp.where(first, accs[0], accs[1]).astype(BF16)


def _sb_attention(planes, *, first_plane):
    _, b, s, _ = planes.shape
    hp = SB_HEADS // SB_HEADS_PER_PLANE
    tq = SB_BLOCK
    return pl.pallas_call(
        _sb_kernel,
        grid=(b, hp, s // tq),
        in_specs=[pl.BlockSpec((None, None, tq, LANES), lambda bi, hi, qi: (first_plane + hi, bi, qi, 0)),
                  pl.BlockSpec((None, None, s, LANES), lambda bi, hi, qi: (first_plane + hp + hi, bi, 0, 0)),
                  pl.BlockSpec((None, None, s, LANES), lambda bi, hi, qi: (first_plane + 2 * hp + hi, bi, 0, 0))],
        out_specs=pl.BlockSpec((None, None, tq, LANES), lambda bi, hi, qi: (hi, bi, qi, 0)),
        out_shape=jax.ShapeDtypeStruct((hp, b, s, LANES), BF16),
        compiler_params=_cparams(("parallel", "parallel", "arbitrary")),
        name="stickbreak",
    )(planes, planes, planes)


def _pack_pair(x):
    n = x.shape[1] // 2
    lo = lax.bitcast_convert_type(x[:, :n].astype(BF16).astype(F32), jnp.int32)
    hi = lax.bitcast_convert_type(x[:, n:].astype(BF16).astype(F32), jnp.int32)
    return lax.shift_right_logical(lo, 16) | (hi & HIGH_HALF_MASK)


def _unpack_pair(p):
    lo = lax.bitcast_convert_type(lax.shift_left(p, 16), F32)
    hi = lax.bitcast_convert_type(p & HIGH_HALF_MASK, F32)
    return lo, hi


def _route(h, wr, run_ref):
    tm = h.shape[0]
    logits = jnp.dot(h, wr, preferred_element_type=F32, precision=lax.Precision.HIGHEST)
    lane = lax.broadcasted_iota(jnp.int32, (1, LANES), 1).astype(F32)
    neg = jnp.float32(-jnp.inf)
    lg = jnp.where(lane < N_EXPERTS, logits, neg)
    m1 = jnp.max(lg, axis=-1, keepdims=True)
    i1 = jnp.min(jnp.where(lg == m1, lane, float(LANES)), axis=-1, keepdims=True)
    sel1 = lane == i1
    lg2 = jnp.where(sel1, neg, lg)
    m2 = jnp.max(lg2, axis=-1, keepdims=True)
    i2 = jnp.min(jnp.where(lg2 == m2, lane, float(LANES)), axis=-1, keepdims=True)
    sel2 = lane == i2
    e2 = jnp.exp(m2 - m1)
    w1 = 1.0 / (1.0 + e2)
    w2 = e2 * w1

    both = jnp.where(jnp.logical_or(sel1, sel2), 1.0, 0.0)
    row = lax.broadcasted_iota(jnp.int32, (tm, tm), 0)
    col = lax.broadcasted_iota(jnp.int32, (tm, tm), 1)
    before = jnp.where(col < row, 1.0, 0.0).astype(BF16)
    cnt = run_ref[...] + _dot(before, both.astype(BF16))
    rank1 = jnp.sum(jnp.where(sel1, cnt, 0.0), axis=-1, keepdims=True)
    rank2 = jnp.sum(jnp.where(sel2, cnt, 0.0), axis=-1, keepdims=True)
    run_ref[...] = run_ref[...] + jnp.sum(both, axis=0, keepdims=True)

    meta = jnp.zeros((tm, LANES), F32)
    for k, v in enumerate((w1, w2, i1, i2, rank1, rank2)):
        meta = jnp.where(lane == float(k), v, meta)
    return meta


def _combine_kernel(*refs, with_router):
    if with_router:
        (x_ref, oa_ref, ob_ref, ga_ref, gb_ref, pa_ref, pb_ref, wo_ref, g2_ref, wr_ref,
         xo_ref, h2_ref, meta_ref, tot_ref, run_ref) = refs
    else:
        x_ref, oa_ref, ob_ref, ga_ref, gb_ref, pa_ref, pb_ref, wo_ref, g2_ref, xo_ref, h2_ref = refs

    def cat(ref):
        return jnp.concatenate([ref[c] for c in range(ref.shape[0])], axis=1)

    ya = _dot(cat(oa_ref), pa_ref[...])
    yb = _dot(cat(ob_ref), pb_ref[...])
    y = _sigmoid(cat(ga_ref).astype(F32)) * ya + _sigmoid(cat(gb_ref).astype(F32)) * yb
    xn = x_ref[...] + _dot(y.astype(BF16), wo_ref[...])
    xo_ref[...] = xn
    h2 = _rms(xn, g2_ref[...])
    if with_router:
        @pl.when(pl.program_id(0) == 0)
        def _():
            run_ref[...] = jnp.zeros_like(run_ref)

        h2_ref[...] = _pack_pair(h2)
        meta_ref[...] = _route(h2, wr_ref[...], run_ref)
        tot_ref[...] = jnp.broadcast_to(run_ref[...], tot_ref.shape)
    else:
        h2_ref[...] = h2.astype(BF16)


def _combine(x2, oa, ob, planes, pa, pb, wo, g2, wr, *, gate_plane, tm=512):
    t, d = x2.shape
    tm = min(tm, t)
    na, nb = oa.shape[0], ob.shape[0]
    ng = d // LANES
    with_router = wr is not None
    gblk = gate_plane // ng
    in_specs = [pl.BlockSpec((tm, d), lambda i: (i, 0)),
                pl.BlockSpec((na, tm, LANES), lambda i: (0, i, 0)),
                pl.BlockSpec((nb, tm, LANES), lambda i: (0, i, 0)),
                pl.BlockSpec((ng, tm, LANES), lambda i: (gblk, i, 0)),
                pl.BlockSpec((ng, tm, LANES), lambda i: (gblk + 1, i, 0)),
                pl.BlockSpec(pa.shape, lambda i: (0, 0)),
                pl.BlockSpec(pb.shape, lambda i: (0, 0)),
                pl.BlockSpec(wo.shape, lambda i: (0, 0)),
                pl.BlockSpec((1, d), lambda i: (0, 0))]
    args = [x2, oa, ob, planes, planes, pa, pb, wo, g2.reshape(1, d)]
    out_specs = [pl.BlockSpec((tm, d), lambda i: (i, 0))]
    out_shape = [jax.ShapeDtypeStruct((t, d), F32)]
    scratch = []
    if with_router:
        in_specs.append(pl.BlockSpec(wr.shape, lambda i: (0, 0)))
        args.append(wr)
        out_specs += [pl.BlockSpec((tm, d // 2), lambda i: (i, 0)),
                      pl.BlockSpec((tm, LANES), lambda i: (i, 0)),
                      pl.BlockSpec((SUBLANES, LANES), lambda i: (0, 0))]
        out_shape += [jax.ShapeDtypeStruct((t, d // 2), jnp.int32),
                      jax.ShapeDtypeStruct((t, LANES), F32),
                      jax.ShapeDtypeStruct((SUBLANES, LANES), F32)]
        scratch = [pltpu.VMEM((1, LANES), F32)]
    else:
        out_specs.append(pl.BlockSpec((tm, d), lambda i: (i, 0)))
        out_shape.append(jax.ShapeDtypeStruct((t, d), BF16))
    return pl.pallas_call(
        functools.partial(_combine_kernel, with_router=with_router),
        grid=(t // tm,),
        in_specs=in_specs, out_specs=out_specs, out_shape=out_shape, scratch_shapes=scratch,
        compiler_params=_cparams(("arbitrary",)),
        name="combine_router" if with_router else "combine",
    )(*args)


def _swiglu_kernel(h_ref, x_ref, wg_ref, wu_ref, wd_ref, o_ref, acc_ref):
    j = pl.program_id(1)

    @pl.when(j == 0)
    def _():
        acc_ref[...] = jnp.zeros_like(acc_ref)

    h = h_ref[...]
    a = _dot(h, wg_ref[...])
    u = _dot(h, wu_ref[...])
    acc_ref[...] += _dot((a * _sigmoid(a) * u).astype(BF16), wd_ref[...])

    @pl.when(j == pl.num_programs(1) - 1)
    def _():
        o_ref[...] = x_ref[...] + acc_ref[...]


def _swiglu(h2, x2, wg, wu, wd, *, tm=1024, tf=512):
    t, d = x2.shape
    f = wg.shape[1]
    tm = min(tm, t)
    return pl.pallas_call(
        _swiglu_kernel,
        grid=(t // tm, f // tf),
        in_specs=[pl.BlockSpec((tm, d), lambda i, j: (i, 0)),
                  pl.BlockSpec((tm, d), lambda i, j: (i, 0)),
                  pl.BlockSpec((d, tf), lambda i, j: (0, j)),
                  pl.BlockSpec((d, tf), lambda i, j: (0, j)),
                  pl.BlockSpec((tf, d), lambda i, j: (j, 0))],
        out_specs=pl.BlockSpec((tm, d), lambda i, j: (i, 0)),
        out_shape=jax.ShapeDtypeStruct((t, d), F32),
        scratch_shapes=[pltpu.VMEM((tm, d), F32)],
        compiler_params=_cparams(("parallel", "arbitrary")),
        name="swiglu",
    )(h2, x2, wg, wu, wd)


def _sc_workers():
    info = plsc.get_sparse_core_info()
    return info.num_cores, info.num_cores * info.num_subcores


def _sc_scatter_rows(rows, pos, n_out):
    t, w = rows.shape
    kk = pos.shape[0]
    nc, nw = _sc_workers()
    per_w = t // nw
    nchunk = per_w // SC_CHUNK
    mesh = plsc.VectorSubcoreMesh(core_axis_name="c", subcore_axis_name="s")

    @functools.partial(
        pl.kernel, mesh=mesh, out_type=jax.ShapeDtypeStruct((n_out, w), rows.dtype),
        scratch_types=[pltpu.VMEM((kk, nchunk, SC_CHUNK), jnp.int32), pltpu.VMEM((SC_BUFFERS, SC_CHUNK, w), rows.dtype),
                       pltpu.SemaphoreType.DMA((SC_BUFFERS,)), pltpu.SemaphoreType.DMA((SC_BUFFERS,))],
        name="moe_dispatch")
    def scatter_kernel(rows_hbm, pos_hbm, out_hbm, pos_v, rows_v, load_sem, store_sem):
        wid = lax.axis_index("s") * nc + lax.axis_index("c")
        base = wid * per_w
        pltpu.sync_copy(pos_hbm.at[wid], pos_v)

        def load(j, b):
            src = rows_hbm.at[pl.ds(pl.multiple_of(base + j * SC_CHUNK, SC_CHUNK), SC_CHUNK)]
            return pltpu.make_async_copy(src, rows_v.at[b], load_sem.at[b])

        def store(j, b, k):
            return pltpu.make_async_copy(rows_v.at[b], out_hbm.at[pos_v.at[k, j]], store_sem.at[b])

        for b in range(min(SC_BUFFERS, nchunk)):
            load(b, b).start()

        @pl.loop(0, nchunk, step=SC_BUFFERS)
        def _(g):
            for b in range(SC_BUFFERS):
                j = g + b

                @pl.when(j < nchunk)
                def _():
                    load(j, b).wait()
                    for k in range(kk):
                        store(j, b, k).start()
                    for k in range(kk):
                        store(j, b, k).wait()

                    @pl.when(j + SC_BUFFERS < nchunk)
                    def _():
                        load(j + SC_BUFFERS, b).start()

    return scatter_kernel(rows, pos.reshape(kk, nw, nchunk, SC_CHUNK).transpose(1, 0, 2, 3))


def _sc_gather_rows(table, idx):
    n = idx.shape[0]
    w = table.shape[1]
    nc, nw = _sc_workers()
    per_w = n // nw
    nchunk = per_w // SC_CHUNK
    mesh = plsc.VectorSubcoreMesh(core_axis_name="c", subcore_axis_name="s")

    @functools.partial(
        pl.kernel, mesh=mesh, out_type=jax.ShapeDtypeStruct((n, w), table.dtype),
        scratch_types=[pltpu.VMEM((nchunk, SC_CHUNK), jnp.int32), pltpu.VMEM((SC_BUFFERS, SC_CHUNK, w), table.dtype),
                       pltpu.SemaphoreType.DMA((SC_BUFFERS,)), pltpu.SemaphoreType.DMA((SC_BUFFERS,))],
        name="moe_collect")
    def gather_kernel(table_hbm, idx_hbm, out_hbm, idx_v, rows_v, gather_sem, store_sem):
        wid = lax.axis_index("s") * nc + lax.axis_index("c")
        base = wid * per_w
        pltpu.sync_copy(idx_hbm.at[wid], idx_v)

        def gather(j, b):
            return pltpu.make_async_copy(table_hbm.at[idx_v.at[j]], rows_v.at[b], gather_sem.at[b])

        def store(j, b):
            dst = out_hbm.at[pl.ds(pl.multiple_of(base + j * SC_CHUNK, SC_CHUNK), SC_CHUNK)]
            return pltpu.make_async_copy(rows_v.at[b], dst, store_sem.at[b])

        for b in range(min(SC_BUFFERS, nchunk)):
            gather(b, b).start()

        @pl.loop(0, nchunk, step=SC_BUFFERS)
        def _(g):
            for b in range(SC_BUFFERS):
                j = g + b

                @pl.when(j < nchunk)
                def _():
                    gather(j, b).wait()
                    store(j, b).start()
                    store(j, b).wait()

                    @pl.when(j + SC_BUFFERS < nchunk)
                    def _():
                        gather(j + SC_BUFFERS, b).start()

    return gather_kernel(table, idx.reshape(nw, nchunk, SC_CHUNK))


def _experts_kernel(te_ref, nu_ref, xs_ref, wg_ref, wu_ref, wd_ref, ys_ref, h_scr, acc_ref):
    i = pl.program_id(0)
    j = pl.program_id(1)

    @pl.when(i < nu_ref[0])
    def _():
        @pl.when(j == 0)
        def _():
            lo, hi = _unpack_pair(xs_ref[...])
            half = lo.shape[1]
            h_scr[:, :half] = lo.astype(BF16)
            h_scr[:, half:] = hi.astype(BF16)
            acc_ref[...] = jnp.zeros_like(acc_ref)

        h = h_scr[...]
        a = _dot(h, wg_ref[...])
        u = _dot(h, wu_ref[...])
        acc_ref[...] += _dot((a * _sigmoid(a) * u).astype(BF16), wd_ref[...])

        @pl.when(j == pl.num_programs(1) - 1)
        def _():
            ys_ref[...] = _pack_pair(acc_ref[...])


def _experts(xs, tile_expert, n_used, wg, wu, wd, *, tm, tf=512):
    r, half = xs.shape
    d = 2 * half
    f = wg.shape[2]
    grid_spec = pltpu.PrefetchScalarGridSpec(
        num_scalar_prefetch=2,
        grid=(r // tm, f // tf),
        in_specs=[pl.BlockSpec((tm, half), lambda i, j, te, nu: (i, 0)),
                  pl.BlockSpec((None, d, tf), lambda i, j, te, nu: (te[i], 0, j)),
                  pl.BlockSpec((None, d, tf), lambda i, j, te, nu: (te[i], 0, j)),
                  pl.BlockSpec((None, tf, d), lambda i, j, te, nu: (te[i], j, 0))],
        out_specs=pl.BlockSpec((tm, half), lambda i, j, te, nu: (i, 0)),
        scratch_shapes=[pltpu.VMEM((tm, d), BF16), pltpu.VMEM((tm, d), F32)])
    return pl.pallas_call(
        _experts_kernel,
        grid_spec=grid_spec,
        out_shape=jax.ShapeDtypeStruct((r, half), jnp.int32),
        compiler_params=_cparams(("arbitrary", "arbitrary")),
        name="moe_experts",
    )(tile_expert, n_used, xs, wg, wu, wd)


def _moe_out_kernel(*refs, final_norm):
    if final_norm:
        x_ref, y1_ref, y2_ref, meta_ref, gf_ref, o_ref = refs
    else:
        x_ref, y1_ref, y2_ref, meta_ref, o_ref = refs
    meta = meta_ref[...]
    w1, w2 = meta[:, 0:1], meta[:, 1:2]
    lo1, hi1 = _unpack_pair(y1_ref[...])
    lo2, hi2 = _unpack_pair(y2_ref[...])
    half = lo1.shape[1]
    x = x_ref[...]
    y = jnp.concatenate([x[:, :half] + w1 * lo1 + w2 * lo2, x[:, half:] + w1 * hi1 + w2 * hi2], axis=1)
    if final_norm:
        y = _rms(y, gf_ref[...])
    o_ref[...] = y


def _moe_out(x2, yg, meta, gf, *, tm=512):
    t, d = x2.shape
    tm = min(tm, t)
    nblk = t // tm
    final_norm = gf is not None
    in_specs = [pl.BlockSpec((tm, d), lambda i: (i, 0)),
                pl.BlockSpec((tm, d // 2), lambda i: (i, 0)),
                pl.BlockSpec((tm, d // 2), lambda i: (i + nblk, 0)),
                pl.BlockSpec((tm, LANES), lambda i: (i, 0))]
    args = [x2, yg, yg, meta]
    if final_norm:
        in_specs.append(pl.BlockSpec((1, d), lambda i: (0, 0)))
        args.append(gf.reshape(1, d))
    return pl.pallas_call(
        functools.partial(_moe_out_kernel, final_norm=final_norm),
        grid=(nblk,),
        in_specs=in_specs,
        out_specs=pl.BlockSpec((tm, d), lambda i: (i, 0)),
        out_shape=jax.ShapeDtypeStruct((t, d), F32),
        compiler_params=_cparams(("parallel",)),
        name="moe_out",
    )(*args)


def _moe(h2p, xn, meta, totals, wg, wu, wd, gf, *, tm=MOE_TILE):
    t = xn.shape[0]
    tm = min(tm, t)
    ne = wg.shape[0]
    cnt = totals[0, :ne].astype(jnp.int32)
    cap = ((cnt + tm - 1) // tm) * tm
    ends = jnp.cumsum(cap)
    off = ends - cap
    n_tiles = 2 * t // tm + ne
    tile_start = jnp.arange(n_tiles, dtype=jnp.int32) * tm
    tile_expert = jnp.minimum(jnp.sum(tile_start[:, None] >= ends[None, :], axis=1), ne - 1).astype(jnp.int32)
    n_used = (ends[-1:] // tm).astype(jnp.int32)
    e12 = meta[:, 2:4].astype(jnp.int32)
    pos = (off[e12] + meta[:, 4:6].astype(jnp.int32)).T
    xs = _sc_scatter_rows(h2p, pos, n_tiles * tm)
    ys = _experts(xs, tile_expert, n_used, wg, wu, wd, tm=tm)
    yg = _sc_gather_rows(ys, pos.reshape(2 * t))
    return _moe_out(xn, yg, meta, gf)


def _final_norm_kernel(x_ref, g_ref, o_ref):
    o_ref[...] = _rms(x_ref[...], g_ref[...])


def _final_norm(x2, g, *, tm=512):
    t, d = x2.shape
    tm = min(tm, t)
    return pl.pallas_call(
        _final_norm_kernel,
        grid=(t // tm,),
        in_specs=[pl.BlockSpec((tm, d), lambda i: (i, 0)), pl.BlockSpec((1, d), lambda i: (0, 0))],
        out_specs=pl.BlockSpec((tm, d), lambda i: (i, 0)),
        out_shape=jax.ShapeDtypeStruct((t, d), F32),
        compiler_params=_cparams(("parallel",)),
        name="final_norm",
    )(x2, g.reshape(1, d))


def kernel(x, mix_norm, w_in, hgrn_lb_logits, hgrn_out_norm, w_branch_hgrn, w_branch_sb, w_out, ffn_norm,
           dense_w_gate, dense_w_up, dense_w_down, moe_router, moe_w_gate, moe_w_up, moe_w_down, final_norm):
    b, s, d = x.shape
    t = b * s
    depth = w_in.shape[0]
    hgrn_width = HGRN_HEADS * HGRN_HEAD_DIM
    sb_width = SB_HEADS * SB_HEAD_DIM
    gate_cols = 4 * hgrn_width + 3 * sb_width
    n_gate_planes = 2 * d // LANES
    hgrn_plane = n_gate_planes
    sb_plane = hgrn_plane + 4 * hgrn_width // LANES

    lb_all = jnp.cumsum(jax.nn.softmax(hgrn_lb_logits.astype(F32), axis=0), axis=0)
    lb_all = lb_all - lb_all[0:1]
    lb_params = jnp.stack([jnp.log(lb_all), jnp.log1p(-lb_all), 1.0 - lb_all], axis=1)
    lb_params = lb_params.reshape(depth, 3, HGRN_HEADS, HGRN_HEAD_DIM).transpose(0, 2, 1, 3)

    x2 = x.reshape(t, d)
    for layer in range(depth):
        w = w_in[layer]
        w = jnp.concatenate([w[:, gate_cols:], w[:, :gate_cols]], axis=1).astype(BF16)
        planes = _inproj(x2, mix_norm[layer], w)
        planes4 = planes.reshape(planes.shape[0], b, s, LANES)
        oa = _hgrn(planes4, lb_params[layer], hgrn_out_norm[layer], first_plane=hgrn_plane)
        ob = _sb_attention(planes4, first_plane=sb_plane)
        j = layer // 2
        moe = layer % 2 == 1
        wr = None
        if moe:
            wr = jnp.zeros((d, LANES), F32).at[:, :N_EXPERTS].set(moe_router[j].astype(F32))
        res = _combine(x2, oa.reshape(HGRN_HEADS, t, LANES), ob.reshape(ob.shape[0], t, LANES), planes,
                       w_branch_hgrn[layer].astype(BF16), w_branch_sb[layer].astype(BF16),
                       w_out[layer].astype(BF16), ffn_norm[layer], wr, gate_plane=0)
        last = layer == depth - 1
        if moe:
            xn, h2p, meta, totals = res
            x2 = _moe(h2p, xn, meta, totals, moe_w_gate[j].astype(BF16), moe_w_up[j].astype(BF16),
                      moe_w_down[j].astype(BF16), final_norm if last else None)
        else:
            xn, h2 = res
            x2 = _swiglu(h2, xn, dense_w_gate[j].astype(BF16), dense_w_up[j].astype(BF16),
                         dense_w_down[j].astype(BF16))
            if last:
                x2 = _final_norm(x2, final_norm)
    return x2.reshape(b, s, d)
```

```python
import functools

import jax
import jax.numpy as jnp
from jax import lax
from jax.experimental import pallas as pl
from jax.experimental.pallas import tpu as pltpu
from jax.experimental.pallas import tpu_sc as plsc

F32 = jnp.float32
BF16 = jnp.bfloat16

LANES = 128
SUBLANES = 8
HGRN_HEADS = 4
HGRN_HEAD_DIM = 128
SB_HEADS = 8
SB_HEAD_DIM = 64
SB_HEADS_PER_PLANE = LANES // SB_HEAD_DIM
N_EXPERTS = 8
RMS_EPS = 1e-6
VMEM_LIMIT_BYTES = 56 * 1024 * 1024

HGRN_CHUNK = 128
HGRN_DIAG = 8
HGRN_ROWS = 512
HGRN_HEADS_PER_STEP = 4
SB_BLOCK = 256
FFN_TILE = 896
MOE_TILE = 1024
SC_CHUNK = 64
SC_BUFFERS = 3
HIGH_HALF_MASK = -65536
SB_PLANES = 2
SB_UNDERFLOW_LOG = -105.0


def _cparams(semantics):
    return pltpu.CompilerParams(dimension_semantics=semantics, vmem_limit_bytes=VMEM_LIMIT_BYTES)


def _rms(x, g):
    ms = jnp.mean(x * x, axis=-1, keepdims=True)
    return x * lax.rsqrt(ms + RMS_EPS) * g


def _sigmoid(x):
    return 0.5 * jnp.tanh(0.5 * x) + 0.5


def _dot(a, b):
    return jnp.dot(a, b, preferred_element_type=F32)


def _dot_nt(a, b):
    return lax.dot_general(a, b, (((1,), (1,)), ((), ())), preferred_element_type=F32)


def _dot_tn(a, b):
    return lax.dot_general(a, b, (((0,), (0,)), ((), ())), preferred_element_type=F32)


def _inproj_kernel(x_ref, g_ref, w_ref, o_ref, *, tn):
    h = _rms(x_ref[...], g_ref[...]).astype(BF16)
    for j in range(w_ref.shape[1] // tn):
        r = _dot(h, w_ref[:, j * tn:(j + 1) * tn])
        for c in range(tn // LANES):
            o_ref[j * (tn // LANES) + c] = r[:, c * LANES:(c + 1) * LANES].astype(BF16)


def _inproj(x2, g, w, *, tm=512, tn=512):
    t, d = x2.shape
    n = w.shape[1]
    tm = min(tm, t)
    return pl.pallas_call(
        functools.partial(_inproj_kernel, tn=tn),
        grid=(t // tm,),
        in_specs=[pl.BlockSpec((tm, d), lambda i: (i, 0)),
                  pl.BlockSpec((1, d), lambda i: (0, 0)),
                  pl.BlockSpec((d, n), lambda i: (0, 0))],
        out_specs=pl.BlockSpec((n // LANES, tm, LANES), lambda i: (0, i, 0)),
        out_shape=jax.ShapeDtypeStruct((n // LANES, t, LANES), BF16),
        compiler_params=_cparams(("parallel",)),
        name="inproj",
    )(x2, g.reshape(1, d), w)


def _hgrn_chunk(zq, zf, vi, zog, lbp, gn, st):
    c = zq.shape[0]
    log_lb, log1m_lb, one_m_lb = lbp[0:1], lbp[1:2], lbp[2:3]

    q = zq * _sigmoid(zq)
    log_sig = jnp.minimum(zf, 0.0) - jnp.log(1.0 + jnp.exp(-jnp.abs(zf)))
    k = one_m_lb * _sigmoid(-zf)
    cc = log1m_lb + log_sig
    log_f = jnp.maximum(log_lb, cc) + jnp.log(1.0 + jnp.exp(-jnp.abs(log_lb - cc)))

    hi = log_f.astype(BF16)
    r1 = log_f - hi.astype(F32)
    mid = r1.astype(BF16)
    lo = (r1 - mid.astype(F32)).astype(BF16)
    row = lax.broadcasted_iota(jnp.int32, (c, c), 0)
    col = lax.broadcasted_iota(jnp.int32, (c, c), 1)
    tri = jnp.where(col <= row, 1.0, 0.0).astype(BF16)
    parts = _dot(tri, jnp.concatenate([hi, mid, lo], axis=1))
    b = parts[:, :LANES] + parts[:, LANES:2 * LANES] + parts[:, 2 * LANES:]
    b_prev = b - log_f
    b_last = b[c - 1:c, :]

    o = _dot_nt((q * jnp.exp(b)).astype(BF16), st.astype(BF16))
    k_tail = (k * jnp.exp(b_last - b)).astype(BF16)
    st_new = st * jnp.exp(b_last) + _dot_tn(vi, k_tail)

    rows = lax.broadcasted_iota(jnp.int32, (c, 1), 0)
    scores = jnp.zeros((c, c), F32)
    m = HGRN_DIAG
    while m < c:
        n = c // m
        base = b_prev.reshape(n, m, LANES)[:, 0:1, :]
        p = b.reshape(n, m, LANES) - base
        suf = (p[:, m - 1:m, :] - p).reshape(c, LANES)
        p = p.reshape(c, LANES)
        odd = ((rows // m) % 2) == 1
        q_m = jnp.where(odd, q * jnp.exp(p), 0.0).astype(BF16)
        k_m = jnp.where(odd, 0.0, k * jnp.exp(suf)).astype(BF16)
        s_m = _dot_nt(q_m, k_m)
        scores = scores + jnp.where((row // (2 * m)) == (col // (2 * m)), s_m, 0.0)
        m *= 2
    o = o + _dot(scores.astype(BF16), vi)

    blocks = (c // HGRN_DIAG, HGRN_DIAG, LANES)
    q3, k3, b3, v3 = (a.reshape(blocks) for a in (q, k, b, vi.astype(F32)))
    tmod = lax.broadcasted_iota(jnp.int32, (1, HGRN_DIAG, 1), 1)
    o3 = jnp.zeros(blocks, F32)
    for j in range(HGRN_DIAG):
        if j == 0:
            ks, bs, vs = k3, b3, v3
        else:
            ks, bs, vs = pltpu.roll(k3, j, 1), pltpu.roll(b3, j, 1), pltpu.roll(v3, j, 1)
        s = jnp.sum(q3 * ks * jnp.exp(b3 - bs), axis=-1, keepdims=True)
        o3 = o3 + jnp.where(tmod >= j, s, 0.0) * vs
    o = o + o3.reshape(c, LANES)

    out = _rms(o, gn) * (zog * _sigmoid(zog))
    return out.astype(BF16), st_new


def _hgrn_kernel(q_ref, f_ref, i_ref, og_ref, lb_ref, gn_ref, o_ref, st_ref):
    @pl.when(pl.program_id(2) == 0)
    def _():
        st_ref[...] = jnp.zeros_like(st_ref)

    gn = gn_ref[...]

    def body(i, carry):
        rows = pl.ds(pl.multiple_of(i * HGRN_CHUNK, HGRN_CHUNK), HGRN_CHUNK)
        for h in range(q_ref.shape[0]):
            out, st_new = _hgrn_chunk(q_ref[h, rows, :].astype(F32), f_ref[h, rows, :].astype(F32), i_ref[h, rows, :],
                                      og_ref[h, rows, :].astype(F32), lb_ref[h], gn, st_ref[h])
            o_ref[h, rows, :] = out
            st_ref[h] = st_new
        return carry

    lax.fori_loop(0, q_ref.shape[1] // HGRN_CHUNK, body, 0)


def _hgrn(planes, lb_params, gn, *, first_plane):
    _, b, s, _ = planes.shape
    rows = min(HGRN_ROWS, s)
    h = HGRN_HEADS
    n = HGRN_HEADS_PER_STEP

    def plane_spec(k):
        blk = (first_plane + k * h) // n
        return pl.BlockSpec((n, None, rows, LANES), lambda bi, hi, ci: (blk + hi, bi, ci, 0))

    return pl.pallas_call(
        _hgrn_kernel,
        grid=(b, h // n, s // rows),
        in_specs=[plane_spec(0), plane_spec(1), plane_spec(2), plane_spec(3),
                  pl.BlockSpec((n, 3, LANES), lambda bi, hi, ci: (hi, 0, 0)),
                  pl.BlockSpec((1, LANES), lambda bi, hi, ci: (0, 0))],
        out_specs=pl.BlockSpec((n, None, rows, LANES), lambda bi, hi, ci: (hi, bi, ci, 0)),
        out_shape=jax.ShapeDtypeStruct((h, b, s, LANES), BF16),
        scratch_shapes=[pltpu.VMEM((n, HGRN_HEAD_DIM, HGRN_HEAD_DIM), F32)],
        compiler_params=_cparams(("parallel", "parallel", "arbitrary")),
        name="hgrn2",
    )(planes, planes, planes, planes, lb_params, gn.reshape(1, LANES))


def _sb_kernel(q_ref, k_ref, v_ref, o_ref):
    n_planes, tq, _ = q_ref.shape
    tk = tq
    qi = pl.program_id(2)
    lane = lax.broadcasted_iota(jnp.int32, (1, LANES), 1)
    first = lane < SB_HEAD_DIM
    q_heads = []
    for p in range(n_planes):
        q2 = q_ref[p] * jnp.asarray(SB_HEAD_DIM ** -0.5, BF16)
        q_heads += [(p, jnp.where(first, q2, jnp.zeros_like(q2))), (p, jnp.where(first, jnp.zeros_like(q2), q2))]
    n_heads = len(q_heads)
    strict = lax.broadcasted_iota(jnp.int32, (tq, tk), 1) < lax.broadcasted_iota(jnp.int32, (tq, tk), 0)
    later = jnp.where(lax.broadcasted_iota(jnp.int32, (tk, tk), 0) > lax.broadcasted_iota(jnp.int32, (tk, tk), 1),
                      1.0, 0.0).astype(BF16)

    def block(kb, accs, sums, diagonal):
        rows = pl.ds(pl.multiple_of(kb * tk, tk), tk)
        new_accs, new_sums = [], []
        for h, (p, qh) in enumerate(q_heads):
            z = _dot_nt(qh, k_ref[p, rows, :])
            neg_z = -z
            log_keep = jnp.minimum(neg_z, 0.0) - jnp.log(1.0 + jnp.exp(jnp.minimum(z, neg_z)))
            if diagonal:
                log_keep = jnp.where(strict, log_keep, 0.0)
            after = _dot(log_keep.astype(BF16), later)
            w = jnp.exp(z + log_keep + after + sums[h])
            if diagonal:
                w = jnp.where(strict, w, 0.0)
            new_accs.append(accs[h] + _dot(w.astype(BF16), v_ref[p, rows, :]))
            new_sums.append(sums[h] + jnp.sum(log_keep, axis=-1, keepdims=True))
        top = new_sums[0]
        for h in range(1, n_heads):
            top = jnp.maximum(top, new_sums[h])
        return tuple(new_accs), tuple(new_sums), (jnp.max(top) < SB_UNDERFLOW_LOG).astype(jnp.int32)

    zero_acc = jnp.zeros((tq, LANES), F32)
    zero_sum = jnp.zeros((tq, 1), F32)
    accs, sums, done = block(qi, (zero_acc,) * n_heads, (zero_sum,) * n_heads, True)

    def cond(carry):
        it, done, _, _ = carry
        return jnp.logical_and(it <= qi, done == 0)

    def body(carry):
        it, _, accs, sums = carry
        accs, sums, done = block(qi - it, accs, sums, False)
        return it + 1, done, accs, sums

    _, _, accs, _ = lax.while_loop(cond, body, (jnp.int32(1), done, accs, sums))
    for p in range(n_planes):
        o_ref[p] = jnp.where(first, accs[2 * p], accs[2 * p + 1]).astype(BF16)


def _sb_attention(planes, *, first_plane):
    _, b, s, _ = planes.shape
    hp = SB_HEADS // SB_HEADS_PER_PLANE
    tq = SB_BLOCK
    n = SB_PLANES

    def spec(rows, k):
        blk = (first_plane + k * hp) // n
        return pl.BlockSpec((n, None, rows, LANES), lambda bi, hi, qi: (blk + hi, bi, qi if rows == tq else 0, 0))

    return pl.pallas_call(
        _sb_kernel,
        grid=(b, hp // n, s // tq),
        in_specs=[spec(tq, 0), spec(s, 1), spec(s, 2)],
        out_specs=pl.BlockSpec((n, None, tq, LANES), lambda bi, hi, qi: (hi, bi, qi, 0)),
        out_shape=jax.ShapeDtypeStruct((hp, b, s, LANES), BF16),
        compiler_params=_cparams(("parallel", "parallel", "arbitrary")),
        name="stickbreak",
    )(planes, planes, planes)


def _pack_pair(x):
    n = x.shape[1] // 2
    lo = lax.bitcast_convert_type(x[:, :n].astype(BF16).astype(F32), jnp.int32)
    hi = lax.bitcast_convert_type(x[:, n:].astype(BF16).astype(F32), jnp.int32)
    return lax.shift_right_logical(lo, 16) | (hi & HIGH_HALF_MASK)


def _unpack_pair(p):
    lo = lax.bitcast_convert_type(lax.shift_left(p, 16), F32)
    hi = lax.bitcast_convert_type(p & HIGH_HALF_MASK, F32)
    return lo, hi


def _route(h, wr, run_ref):
    tm = h.shape[0]
    h_hi = h.astype(BF16)
    h_lo = (h - h_hi.astype(F32)).astype(BF16)
    logits = _dot(h_hi, wr[0]) + (_dot(h_lo, wr[0]) + _dot(h_hi, wr[1]))
    lane = lax.broadcasted_iota(jnp.int32, (1, LANES), 1).astype(F32)
    neg = jnp.float32(-jnp.inf)
    lg = jnp.where(lane < N_EXPERTS, logits, neg)
    m1 = jnp.max(lg, axis=-1, keepdims=True)
    i1 = jnp.min(jnp.where(lg == m1, lane, float(LANES)), axis=-1, keepdims=True)
    sel1 = lane == i1
    lg2 = jnp.where(sel1, neg, lg)
    m2 = jnp.max(lg2, axis=-1, keepdims=True)
    i2 = jnp.min(jnp.where(lg2 == m2, lane, float(LANES)), axis=-1, keepdims=True)
    sel2 = lane == i2
    e2 = jnp.exp(m2 - m1)
    w1 = 1.0 / (1.0 + e2)
    w2 = e2 * w1

    both = jnp.where(jnp.logical_or(sel1, sel2), 1.0, 0.0)
    row = lax.broadcasted_iota(jnp.int32, (tm, tm), 0)
    col = lax.broadcasted_iota(jnp.int32, (tm, tm), 1)
    before = jnp.where(col < row, 1.0, 0.0).astype(BF16)
    cnt = run_ref[...] + _dot(before, both.astype(BF16))
    rank1 = jnp.sum(jnp.where(sel1, cnt, 0.0), axis=-1, keepdims=True)
    rank2 = jnp.sum(jnp.where(sel2, cnt, 0.0), axis=-1, keepdims=True)
    run_ref[...] = run_ref[...] + jnp.sum(both, axis=0, keepdims=True)

    meta = jnp.zeros((tm, LANES), F32)
    for k, v in enumerate((w1, w2, i1, i2, rank1, rank2)):
        meta = jnp.where(lane == float(k), v, meta)
    return meta


def _combine_kernel(*refs, with_router):
    if with_router:
        (x_ref, oa_ref, ob_ref, ga_ref, gb_ref, pa_ref, pb_ref, wo_ref, g2_ref, wr_ref,
         xo_ref, h2_ref, meta_ref, tot_ref, run_ref) = refs
    else:
        x_ref, oa_ref, ob_ref, ga_ref, gb_ref, pa_ref, pb_ref, wo_ref, g2_ref, xo_ref, h2_ref = refs

    def cat(ref):
        return jnp.concatenate([ref[c] for c in range(ref.shape[0])], axis=1)

    ya = _dot(cat(oa_ref), pa_ref[...])
    yb = _dot(cat(ob_ref), pb_ref[...])
    y = _sigmoid(cat(ga_ref).astype(F32)) * ya + _sigmoid(cat(gb_ref).astype(F32)) * yb
    xn = x_ref[...] + _dot(y.astype(BF16), wo_ref[...])
    xo_ref[...] = xn
    h2 = _rms(xn, g2_ref[...])
    if with_router:
        @pl.when(pl.program_id(0) == 0)
        def _():
            run_ref[...] = jnp.zeros_like(run_ref)

        h2_ref[...] = _pack_pair(h2)
        meta_ref[...] = _route(h2, wr_ref[...], run_ref)
        tot_ref[...] = jnp.broadcast_to(run_ref[...], tot_ref.shape)
    else:
        h2_ref[...] = h2.astype(BF16)


def _combine(x2, oa, ob, planes, pa, pb, wo, g2, wr, *, gate_plane, tm=512):
    t, d = x2.shape
    tm = min(tm, t)
    na, nb = oa.shape[0], ob.shape[0]
    ng = d // LANES
    with_router = wr is not None
    gblk = gate_plane // ng
    in_specs = [pl.BlockSpec((tm, d), lambda i: (i, 0)),
                pl.BlockSpec((na, tm, LANES), lambda i: (0, i, 0)),
                pl.BlockSpec((nb, tm, LANES), lambda i: (0, i, 0)),
                pl.BlockSpec((ng, tm, LANES), lambda i: (gblk, i, 0)),
                pl.BlockSpec((ng, tm, LANES), lambda i: (gblk + 1, i, 0)),
                pl.BlockSpec(pa.shape, lambda i: (0, 0)),
                pl.BlockSpec(pb.shape, lambda i: (0, 0)),
                pl.BlockSpec(wo.shape, lambda i: (0, 0)),
                pl.BlockSpec((1, d), lambda i: (0, 0))]
    args = [x2, oa, ob, planes, planes, pa, pb, wo, g2.reshape(1, d)]
    out_specs = [pl.BlockSpec((tm, d), lambda i: (i, 0))]
    out_shape = [jax.ShapeDtypeStruct((t, d), F32)]
    scratch = []
    if with_router:
        in_specs.append(pl.BlockSpec(wr.shape, lambda i: (0, 0, 0)))
        args.append(wr)
        out_specs += [pl.BlockSpec((tm, d // 2), lambda i: (i, 0)),
                      pl.BlockSpec((tm, LANES), lambda i: (i, 0)),
                      pl.BlockSpec((SUBLANES, LANES), lambda i: (0, 0))]
        out_shape += [jax.ShapeDtypeStruct((t, d // 2), jnp.int32),
                      jax.ShapeDtypeStruct((t, LANES), F32),
                      jax.ShapeDtypeStruct((SUBLANES, LANES), F32)]
        scratch = [pltpu.VMEM((1, LANES), F32)]
    else:
        out_specs.append(pl.BlockSpec((tm, d), lambda i: (i, 0)))
        out_shape.append(jax.ShapeDtypeStruct((t, d), BF16))
    return pl.pallas_call(
        functools.partial(_combine_kernel, with_router=with_router),
        grid=(t // tm,),
        in_specs=in_specs, out_specs=out_specs, out_shape=out_shape, scratch_shapes=scratch,
        compiler_params=_cparams(("arbitrary",)),
        name="combine_router" if with_router else "combine",
    )(*args)


def _swiglu_kernel(h_ref, x_ref, wg_ref, wu_ref, wd_ref, o_ref, acc_ref):
    j = pl.program_id(1)

    @pl.when(j == 0)
    def _():
        acc_ref[...] = jnp.zeros_like(acc_ref)

    h = h_ref[...]
    a = _dot(h, wg_ref[...])
    u = _dot(h, wu_ref[...])
    acc_ref[...] += _dot((a * _sigmoid(a) * u).astype(BF16), wd_ref[...])

    @pl.when(j == pl.num_programs(1) - 1)
    def _():
        o_ref[...] = x_ref[...] + acc_ref[...]


def _swiglu(h2, x2, wg, wu, wd, *, tm=1024, tf=FFN_TILE):
    t, d = x2.shape
    f = wg.shape[1]
    tm = min(tm, t)
    return pl.pallas_call(
        _swiglu_kernel,
        grid=(t // tm, f // tf),
        in_specs=[pl.BlockSpec((tm, d), lambda i, j: (i, 0)),
                  pl.BlockSpec((tm, d), lambda i, j: (i, 0)),
                  pl.BlockSpec((d, tf), lambda i, j: (0, j)),
                  pl.BlockSpec((d, tf), lambda i, j: (0, j)),
                  pl.BlockSpec((tf, d), lambda i, j: (j, 0))],
        out_specs=pl.BlockSpec((tm, d), lambda i, j: (i, 0)),
        out_shape=jax.ShapeDtypeStruct((t, d), F32),
        scratch_shapes=[pltpu.VMEM((tm, d), F32)],
        compiler_params=_cparams(("parallel", "arbitrary")),
        name="swiglu",
    )(h2, x2, wg, wu, wd)


def _sc_workers():
    info = plsc.get_sparse_core_info()
    return info.num_cores, info.num_cores * info.num_subcores


def _sc_scatter_rows(rows, pos, n_out):
    t, w = rows.shape
    kk = pos.shape[0]
    nc, nw = _sc_workers()
    per_w = t // nw
    nchunk = per_w // SC_CHUNK
    mesh = plsc.VectorSubcoreMesh(core_axis_name="c", subcore_axis_name="s")

    @functools.partial(
        pl.kernel, mesh=mesh, out_type=jax.ShapeDtypeStruct((n_out, w), rows.dtype),
        scratch_types=[pltpu.VMEM((kk, nchunk, SC_CHUNK), jnp.int32), pltpu.VMEM((SC_BUFFERS, SC_CHUNK, w), rows.dtype),
                       pltpu.SemaphoreType.DMA((SC_BUFFERS,)), pltpu.SemaphoreType.DMA((SC_BUFFERS,))],
        name="moe_dispatch")
    def scatter_kernel(rows_hbm, pos_hbm, out_hbm, pos_v, rows_v, load_sem, store_sem):
        wid = lax.axis_index("s") * nc + lax.axis_index("c")
        base = wid * per_w
        pltpu.sync_copy(pos_hbm.at[wid], pos_v)

        def load(j, b):
            src = rows_hbm.at[pl.ds(pl.multiple_of(base + j * SC_CHUNK, SC_CHUNK), SC_CHUNK)]
            return pltpu.make_async_copy(src, rows_v.at[b], load_sem.at[b])

        def store(j, b, k):
            return pltpu.make_async_copy(rows_v.at[b], out_hbm.at[pos_v.at[k, j]], store_sem.at[b])

        for b in range(min(SC_BUFFERS, nchunk)):
            load(b, b).start()

        @pl.loop(0, nchunk, step=SC_BUFFERS)
        def _(g):
            for b in range(SC_BUFFERS):
                j = g + b

                @pl.when(j < nchunk)
                def _():
                    load(j, b).wait()
                    for k in range(kk):
                        store(j, b, k).start()
                    for k in range(kk):
                        store(j, b, k).wait()

                    @pl.when(j + SC_BUFFERS < nchunk)
                    def _():
                        load(j + SC_BUFFERS, b).start()

    return scatter_kernel(rows, pos.reshape(kk, nw, nchunk, SC_CHUNK).transpose(1, 0, 2, 3))


def _sc_gather_rows(table, idx):
    n = idx.shape[0]
    w = table.shape[1]
    nc, nw = _sc_workers()
    per_w = n // nw
    nchunk = per_w // SC_CHUNK
    mesh = plsc.VectorSubcoreMesh(core_axis_name="c", subcore_axis_name="s")

    @functools.partial(
        pl.kernel, mesh=mesh, out_type=jax.ShapeDtypeStruct((n, w), table.dtype),
        scratch_types=[pltpu.VMEM((nchunk, SC_CHUNK), jnp.int32), pltpu.VMEM((SC_BUFFERS, SC_CHUNK, w), table.dtype),
                       pltpu.SemaphoreType.DMA((SC_BUFFERS,)), pltpu.SemaphoreType.DMA((SC_BUFFERS,))],
        name="moe_collect")
    def gather_kernel(table_hbm, idx_hbm, out_hbm, idx_v, rows_v, gather_sem, store_sem):
        wid = lax.axis_index("s") * nc + lax.axis_index("c")
        base = wid * per_w
        pltpu.sync_copy(idx_hbm.at[wid], idx_v)

        def gather(j, b):
            return pltpu.make_async_copy(table_hbm.at[idx_v.at[j]], rows_v.at[b], gather_sem.at[b])

        def store(j, b):
            dst = out_hbm.at[pl.ds(pl.multiple_of(base + j * SC_CHUNK, SC_CHUNK), SC_CHUNK)]
            return pltpu.make_async_copy(rows_v.at[b], dst, store_sem.at[b])

        for b in range(min(SC_BUFFERS, nchunk)):
            gather(b, b).start()

        @pl.loop(0, nchunk, step=SC_BUFFERS)
        def _(g):
            for b in range(SC_BUFFERS):
                j = g + b

                @pl.when(j < nchunk)
                def _():
                    gather(j, b).wait()
                    store(j, b).start()
                    store(j, b).wait()

                    @pl.when(j + SC_BUFFERS < nchunk)
                    def _():
                        gather(j + SC_BUFFERS, b).start()

    return gather_kernel(table, idx.reshape(nw, nchunk, SC_CHUNK))


def _experts_kernel(te_ref, nu_ref, xs_ref, wg_ref, wu_ref, wd_ref, ys_ref, h_scr, acc_ref):
    i = pl.program_id(0)
    j = pl.program_id(1)

    @pl.when(i < nu_ref[0])
    def _():
        @pl.when(j == 0)
        def _():
            lo, hi = _unpack_pair(xs_ref[...])
            half = lo.shape[1]
            h_scr[:, :half] = lo.astype(BF16)
            h_scr[:, half:] = hi.astype(BF16)
            acc_ref[...] = jnp.zeros_like(acc_ref)

        h = h_scr[...]
        a = _dot(h, wg_ref[...])
        u = _dot(h, wu_ref[...])
        acc_ref[...] += _dot((a * _sigmoid(a) * u).astype(BF16), wd_ref[...])

        @pl.when(j == pl.num_programs(1) - 1)
        def _():
            ys_ref[...] = _pack_pair(acc_ref[...])


def _experts(xs, tile_expert, n_used, wg, wu, wd, *, tm, tf=FFN_TILE):
    r, half = xs.shape
    d = 2 * half
    f = wg.shape[2]
    grid_spec = pltpu.PrefetchScalarGridSpec(
        num_scalar_prefetch=2,
        grid=(r // tm, f // tf),
        in_specs=[pl.BlockSpec((tm, half), lambda i, j, te, nu: (i, 0)),
                  pl.BlockSpec((None, d, tf), lambda i, j, te, nu: (te[i], 0, j)),
                  pl.BlockSpec((None, d, tf), lambda i, j, te, nu: (te[i], 0, j)),
                  pl.BlockSpec((None, tf, d), lambda i, j, te, nu: (te[i], j, 0))],
        out_specs=pl.BlockSpec((tm, half), lambda i, j, te, nu: (i, 0)),
        scratch_shapes=[pltpu.VMEM((tm, d), BF16), pltpu.VMEM((tm, d), F32)])
    return pl.pallas_call(
        _experts_kernel,
        grid_spec=grid_spec,
        out_shape=jax.ShapeDtypeStruct((r, half), jnp.int32),
        compiler_params=_cparams(("arbitrary", "arbitrary")),
        name="moe_experts",
    )(tile_expert, n_used, xs, wg, wu, wd)


def _moe_out_kernel(*refs, final_norm):
    if final_norm:
        x_ref, y1_ref, y2_ref, meta_ref, gf_ref, o_ref = refs
    else:
        x_ref, y1_ref, y2_ref, meta_ref, o_ref = refs
    meta = meta_ref[...]
    w1, w2 = meta[:, 0:1], meta[:, 1:2]
    lo1, hi1 = _unpack_pair(y1_ref[...])
    lo2, hi2 = _unpack_pair(y2_ref[...])
    half = lo1.shape[1]
    x = x_ref[...]
    y = jnp.concatenate([x[:, :half] + w1 * lo1 + w2 * lo2, x[:, half:] + w1 * hi1 + w2 * hi2], axis=1)
    if final_norm:
        y = _rms(y, gf_ref[...])
    o_ref[...] = y


def _moe_out(x2, yg, meta, gf, *, tm=512):
    t, d = x2.shape
    tm = min(tm, t)
    nblk = t // tm
    final_norm = gf is not None
    in_specs = [pl.BlockSpec((tm, d), lambda i: (i, 0)),
                pl.BlockSpec((tm, d // 2), lambda i: (i, 0)),
                pl.BlockSpec((tm, d // 2), lambda i: (i + nblk, 0)),
                pl.BlockSpec((tm, LANES), lambda i: (i, 0))]
    args = [x2, yg, yg, meta]
    if final_norm:
        in_specs.append(pl.BlockSpec((1, d), lambda i: (0, 0)))
        args.append(gf.reshape(1, d))
    return pl.pallas_call(
        functools.partial(_moe_out_kernel, final_norm=final_norm),
        grid=(nblk,),
        in_specs=in_specs,
        out_specs=pl.BlockSpec((tm, d), lambda i: (i, 0)),
        out_shape=jax.ShapeDtypeStruct((t, d), F32),
        compiler_params=_cparams(("parallel",)),
        name="moe_out",
    )(*args)


def _moe(h2p, xn, meta, totals, wg, wu, wd, gf, *, tm=MOE_TILE):
    t = xn.shape[0]
    tm = min(tm, t)
    ne = wg.shape[0]
    cnt = totals[0, :ne].astype(jnp.int32)
    cap = ((cnt + tm - 1) // tm) * tm
    ends = jnp.cumsum(cap)
    off = ends - cap
    n_tiles = 2 * t // tm + ne
    tile_start = jnp.arange(n_tiles, dtype=jnp.int32) * tm
    tile_expert = jnp.minimum(jnp.sum(tile_start[:, None] >= ends[None, :], axis=1), ne - 1).astype(jnp.int32)
    n_used = (ends[-1:] // tm).astype(jnp.int32)
    e12 = meta[:, 2:4].astype(jnp.int32)
    pos = (off[e12] + meta[:, 4:6].astype(jnp.int32)).T
    xs = _sc_scatter_rows(h2p, pos, n_tiles * tm)
    ys = _experts(xs, tile_expert, n_used, wg, wu, wd, tm=tm)
    yg = _sc_gather_rows(ys, pos.reshape(2 * t))
    return _moe_out(xn, yg, meta, gf)


def _final_norm_kernel(x_ref, g_ref, o_ref):
    o_ref[...] = _rms(x_ref[...], g_ref[...])


def _final_norm(x2, g, *, tm=512):
    t, d = x2.shape
    tm = min(tm, t)
    return pl.pallas_call(
        _final_norm_kernel,
        grid=(t // tm,),
        in_specs=[pl.BlockSpec((tm, d), lambda i: (i, 0)), pl.BlockSpec((1, d), lambda i: (0, 0))],
        out_specs=pl.BlockSpec((tm, d), lambda i: (i, 0)),
        out_shape=jax.ShapeDtypeStruct((t, d), F32),
        compiler_params=_cparams(("parallel",)),
        name="final_norm",
    )(x2, g.reshape(1, d))


def kernel(x, mix_norm, w_in, hgrn_lb_logits, hgrn_out_norm, w_branch_hgrn, w_branch_sb, w_out, ffn_norm,
           dense_w_gate, dense_w_up, dense_w_down, moe_router, moe_w_gate, moe_w_up, moe_w_down, final_norm):
    b, s, d = x.shape
    t = b * s
    depth = w_in.shape[0]
    hgrn_width = HGRN_HEADS * HGRN_HEAD_DIM
    sb_width = SB_HEADS * SB_HEAD_DIM
    gate_cols = 4 * hgrn_width + 3 * sb_width
    n_gate_planes = 2 * d // LANES
    hgrn_plane = n_gate_planes
    sb_plane = hgrn_plane + 4 * hgrn_width // LANES

    lb_all = jnp.cumsum(jax.nn.softmax(hgrn_lb_logits.astype(F32), axis=0), axis=0)
    lb_all = lb_all - lb_all[0:1]
    lb_params = jnp.stack([jnp.log(lb_all), jnp.log1p(-lb_all), 1.0 - lb_all], axis=1)
    lb_params = lb_params.reshape(depth, 3, HGRN_HEADS, HGRN_HEAD_DIM).transpose(0, 2, 1, 3)

    x2 = x.reshape(t, d)
    for layer in range(depth):
        w = w_in[layer]
        w = jnp.concatenate([w[:, gate_cols:], w[:, :gate_cols]], axis=1).astype(BF16)
        planes = _inproj(x2, mix_norm[layer], w)
        planes4 = planes.reshape(planes.shape[0], b, s, LANES)
        oa = _hgrn(planes4, lb_params[layer], hgrn_out_norm[layer], first_plane=hgrn_plane)
        ob = _sb_attention(planes4, first_plane=sb_plane)
        j = layer // 2
        moe = layer % 2 == 1
        wr = None
        if moe:
            wr = jnp.zeros((d, LANES), F32).at[:, :N_EXPERTS].set(moe_router[j].astype(F32))
            wr_hi = wr.astype(BF16)
            wr = jnp.stack([wr_hi, (wr - wr_hi.astype(F32)).astype(BF16)])
        res = _combine(x2, oa.reshape(HGRN_HEADS, t, LANES), ob.reshape(ob.shape[0], t, LANES), planes,
                       w_branch_hgrn[layer].astype(BF16), w_branch_sb[layer].astype(BF16),
                       w_out[layer].astype(BF16), ffn_norm[layer], wr, gate_plane=0)
        last = layer == depth - 1
        if moe:
            xn, h2p, meta, totals = res
            x2 = _moe(h2p, xn, meta, totals, moe_w_gate[j].astype(BF16), moe_w_up[j].astype(BF16),
                      moe_w_down[j].astype(BF16), final_norm if last else None)
        else:
            xn, h2 = res
            x2 = _swiglu(h2, xn, dense_w_gate[j].astype(BF16), dense_w_up[j].astype(BF16),
                         dense_w_down[j].astype(BF16))
            if last:
                x2 = _final_norm(x2, final_norm)
    return x2.reshape(b, s, d)
```

```python
import functools

import jax
import jax.numpy as jnp
from jax import lax
from jax.experimental import pallas as pl
from jax.experimental.pallas import tpu as pltpu
from jax.experimental.pallas import tpu_sc as plsc

F32 = jnp.float32
BF16 = jnp.bfloat16

LANES = 128
SUBLANES = 8
HGRN_HEADS = 4
HGRN_HEAD_DIM = 128
SB_HEADS = 8
SB_HEAD_DIM = 64
SB_HEADS_PER_PLANE = LANES // SB_HEAD_DIM
N_EXPERTS = 8
RMS_EPS = 1e-6
VMEM_LIMIT_BYTES = 56 * 1024 * 1024

HGRN_CHUNK = 128
HGRN_DIAG = 8
HGRN_ROWS = 512
HGRN_HEADS_PER_STEP = 4
SB_BLOCK = 256
FFN_TILE = 512
MOE_TILE = 1024
SC_CHUNK = 64
SC_BUFFERS = 3
HIGH_HALF_MASK = -65536
SB_PLANES = 2
LOG2_E = 1.4426950408889634
SB_UNDERFLOW_LOG = -105.0


def _cparams(semantics):
    return pltpu.CompilerParams(dimension_semantics=semantics, vmem_limit_bytes=VMEM_LIMIT_BYTES)


def _rms(x, g):
    ms = jnp.mean(x * x, axis=-1, keepdims=True)
    return x * lax.rsqrt(ms + RMS_EPS) * g


def _sigmoid(x):
    return 0.5 * jnp.tanh(0.5 * x) + 0.5


def _dot(a, b):
    return jnp.dot(a, b, preferred_element_type=F32)


def _dot_nt(a, b):
    return lax.dot_general(a, b, (((1,), (1,)), ((), ())), preferred_element_type=F32)


def _dot_tn(a, b):
    return lax.dot_general(a, b, (((0,), (0,)), ((), ())), preferred_element_type=F32)


def _inproj_kernel(x_ref, g_ref, w_ref, o_ref, *, tn):
    h = _rms(x_ref[...], g_ref[...]).astype(BF16)
    for j in range(w_ref.shape[1] // tn):
        r = _dot(h, w_ref[:, j * tn:(j + 1) * tn])
        for c in range(tn // LANES):
            o_ref[j * (tn // LANES) + c] = r[:, c * LANES:(c + 1) * LANES].astype(BF16)


def _inproj(x2, g, w, *, tm=512, tn=512):
    t, d = x2.shape
    n = w.shape[1]
    tm = min(tm, t)
    return pl.pallas_call(
        functools.partial(_inproj_kernel, tn=tn),
        grid=(t // tm,),
        in_specs=[pl.BlockSpec((tm, d), lambda i: (i, 0)),
                  pl.BlockSpec((1, d), lambda i: (0, 0)),
                  pl.BlockSpec((d, n), lambda i: (0, 0))],
        out_specs=pl.BlockSpec((n // LANES, tm, LANES), lambda i: (0, i, 0)),
        out_shape=jax.ShapeDtypeStruct((n // LANES, t, LANES), BF16),
        compiler_params=_cparams(("parallel",)),
        name="inproj",
    )(x2, g.reshape(1, d), w)


def _hgrn_chunk(zq, zf, vi, zog, lbp, gn, st):
    c = zq.shape[0]
    log_lb, log1m_lb, one_m_lb = lbp[0:1], lbp[1:2], lbp[2:3]

    q = zq * _sigmoid(zq)
    log_sig = jnp.minimum(zf, 0.0) - jnp.log(1.0 + jnp.exp(-jnp.abs(zf)))
    k = one_m_lb * _sigmoid(-zf)
    cc = log1m_lb + log_sig
    log_f = jnp.maximum(log_lb, cc) + jnp.log(1.0 + jnp.exp(-jnp.abs(log_lb - cc)))
    log_f = log_f * LOG2_E
    yield

    hi = log_f.astype(BF16)
    r1 = log_f - hi.astype(F32)
    mid = r1.astype(BF16)
    lo = (r1 - mid.astype(F32)).astype(BF16)
    row = lax.broadcasted_iota(jnp.int32, (c, c), 0)
    col = lax.broadcasted_iota(jnp.int32, (c, c), 1)
    tri = jnp.where(col <= row, 1.0, 0.0).astype(BF16)
    parts = _dot(tri, jnp.concatenate([hi, mid, lo], axis=1))
    b = parts[:, :LANES] + parts[:, LANES:2 * LANES] + parts[:, 2 * LANES:]
    b_prev = b - log_f
    b_last = b[c - 1:c, :]
    yield

    o = _dot_nt((q * jnp.exp2(b)).astype(BF16), st.astype(BF16))
    k_tail = (k * jnp.exp2(b_last - b)).astype(BF16)
    st_new = st * jnp.exp2(b_last) + _dot_tn(vi, k_tail)
    yield

    rows = lax.broadcasted_iota(jnp.int32, (c, 1), 0)
    scores = jnp.zeros((c, c), F32)
    m = HGRN_DIAG
    while m < c:
        n = c // m
        base = b_prev.reshape(n, m, LANES)[:, 0:1, :]
        p = b.reshape(n, m, LANES) - base
        suf = (p[:, m - 1:m, :] - p).reshape(c, LANES)
        p = p.reshape(c, LANES)
        odd = ((rows // m) % 2) == 1
        q_m = jnp.where(odd, q * jnp.exp2(p), 0.0).astype(BF16)
        k_m = jnp.where(odd, 0.0, k * jnp.exp2(suf)).astype(BF16)
        s_m = _dot_nt(q_m, k_m)
        scores = scores + jnp.where((row // (2 * m)) == (col // (2 * m)), s_m, 0.0)
        m *= 2
        yield
    o = o + _dot(scores.astype(BF16), vi)

    blocks = (c // HGRN_DIAG, HGRN_DIAG, LANES)
    q3, k3, b3, v3 = (a.reshape(blocks) for a in (q, k, b, vi.astype(F32)))
    tmod = lax.broadcasted_iota(jnp.int32, (1, HGRN_DIAG, 1), 1)
    o3 = jnp.zeros(blocks, F32)
    for j in range(HGRN_DIAG):
        if j == 0:
            ks, bs, vs = k3, b3, v3
        else:
            ks, bs, vs = pltpu.roll(k3, j, 1), pltpu.roll(b3, j, 1), pltpu.roll(v3, j, 1)
        s = jnp.sum(q3 * ks * jnp.exp2(b3 - bs), axis=-1, keepdims=True)
        o3 = o3 + jnp.where(tmod >= j, s, 0.0) * vs
        yield
    o = o + o3.reshape(c, LANES)

    out = _rms(o, gn) * (zog * _sigmoid(zog))
    return out.astype(BF16), st_new


def _hgrn_kernel(q_ref, f_ref, i_ref, og_ref, lb_ref, gn_ref, o_ref, st_ref):
    @pl.when(pl.program_id(2) == 0)
    def _():
        st_ref[...] = jnp.zeros_like(st_ref)

    gn = gn_ref[...]

    def body(i, carry):
        rows = pl.ds(pl.multiple_of(i * HGRN_CHUNK, HGRN_CHUNK), HGRN_CHUNK)
        heads = [_hgrn_chunk(q_ref[h, rows, :].astype(F32), f_ref[h, rows, :].astype(F32), i_ref[h, rows, :],
                             og_ref[h, rows, :].astype(F32), lb_ref[h], gn, st_ref[h])
                 for h in range(q_ref.shape[0])]
        pending = dict(enumerate(heads))
        while pending:
            for h, gen in list(pending.items()):
                try:
                    next(gen)
                except StopIteration as done:
                    out, st_new = done.value
                    o_ref[h, rows, :] = out
                    st_ref[h] = st_new
                    del pending[h]
        return carry

    lax.fori_loop(0, q_ref.shape[1] // HGRN_CHUNK, body, 0)


def _hgrn(planes, lb_params, gn, *, first_plane):
    _, b, s, _ = planes.shape
    rows = min(HGRN_ROWS, s)
    h = HGRN_HEADS
    n = HGRN_HEADS_PER_STEP

    def plane_spec(k):
        blk = (first_plane + k * h) // n
        return pl.BlockSpec((n, None, rows, LANES), lambda bi, hi, ci: (blk + hi, bi, ci, 0))

    return pl.pallas_call(
        _hgrn_kernel,
        grid=(b, h // n, s // rows),
        in_specs=[plane_spec(0), plane_spec(1), plane_spec(2), plane_spec(3),
                  pl.BlockSpec((n, 3, LANES), lambda bi, hi, ci: (hi, 0, 0)),
                  pl.BlockSpec((1, LANES), lambda bi, hi, ci: (0, 0))],
        out_specs=pl.BlockSpec((n, None, rows, LANES), lambda bi, hi, ci: (hi, bi, ci, 0)),
        out_shape=jax.ShapeDtypeStruct((h, b, s, LANES), BF16),
        scratch_shapes=[pltpu.VMEM((n, HGRN_HEAD_DIM, HGRN_HEAD_DIM), F32)],
        compiler_params=_cparams(("parallel", "parallel", "arbitrary")),
        name="hgrn2",
    )(planes, planes, planes, planes, lb_params, gn.reshape(1, LANES))


def _sb_kernel(q_ref, k_ref, v_ref, o_ref):
    n_planes, tq, _ = q_ref.shape
    tk = tq
    qi = pl.program_id(2)
    lane = lax.broadcasted_iota(jnp.int32, (1, LANES), 1)
    first = lane < SB_HEAD_DIM
    q_heads = []
    for p in range(n_planes):
        q2 = q_ref[p] * jnp.asarray(SB_HEAD_DIM ** -0.5, BF16)
        q_heads += [(p, jnp.where(first, q2, jnp.zeros_like(q2))), (p, jnp.where(first, jnp.zeros_like(q2), q2))]
    n_heads = len(q_heads)
    strict = lax.broadcasted_iota(jnp.int32, (tq, tk), 1) < lax.broadcasted_iota(jnp.int32, (tq, tk), 0)
    later = jnp.where(lax.broadcasted_iota(jnp.int32, (tk, tk), 0) > lax.broadcasted_iota(jnp.int32, (tk, tk), 1),
                      1.0, 0.0).astype(BF16)

    def block(kb, accs, sums, diagonal):
        rows = pl.ds(pl.multiple_of(kb * tk, tk), tk)
        zs = [_dot_nt(qh, k_ref[p, rows, :]) for p, qh in q_heads]
        log_keeps = []
        for z in zs:
            neg_z = -z
            log_keep = jnp.minimum(neg_z, 0.0) - jnp.log(1.0 + jnp.exp(jnp.minimum(z, neg_z)))
            if diagonal:
                log_keep = jnp.where(strict, log_keep, 0.0)
            log_keeps.append(log_keep)
        afters = [_dot(lk.astype(BF16), later) for lk in log_keeps]
        ws = []
        for h in range(n_heads):
            w = jnp.exp(zs[h] + log_keeps[h] + afters[h] + sums[h])
            if diagonal:
                w = jnp.where(strict, w, 0.0)
            ws.append(w.astype(BF16))
        new_accs = [accs[h] + _dot(ws[h], v_ref[q_heads[h][0], rows, :]) for h in range(n_heads)]
        new_sums = [sums[h] + jnp.sum(log_keeps[h], axis=-1, keepdims=True) for h in range(n_heads)]
        top = new_sums[0]
        for h in range(1, n_heads):
            top = jnp.maximum(top, new_sums[h])
        return tuple(new_accs), tuple(new_sums), (jnp.max(top) < SB_UNDERFLOW_LOG).astype(jnp.int32)

    zero_acc = jnp.zeros((tq, LANES), F32)
    zero_sum = jnp.zeros((tq, 1), F32)
    accs, sums, done = block(qi, (zero_acc,) * n_heads, (zero_sum,) * n_heads, True)

    def cond(carry):
        it, done, _, _ = carry
        return jnp.logical_and(it <= qi, done == 0)

    def body(carry):
        it, _, accs, sums = carry
        accs, sums, done = block(qi - it, accs, sums, False)
        return it + 1, done, accs, sums

    _, _, accs, _ = lax.while_loop(cond, body, (jnp.int32(1), done, accs, sums))
    for p in range(n_planes):
        o_ref[p] = jnp.where(first, accs[2 * p], accs[2 * p + 1]).astype(BF16)


def _sb_attention(planes, *, first_plane):
    _, b, s, _ = planes.shape
    hp = SB_HEADS // SB_HEADS_PER_PLANE
    tq = SB_BLOCK
    n = SB_PLANES

    def spec(rows, k):
        blk = (first_plane + k * hp) // n
        return pl.BlockSpec((n, None, rows, LANES), lambda bi, hi, qi: (blk + hi, bi, qi if rows == tq else 0, 0))

    return pl.pallas_call(
        _sb_kernel,
        grid=(b, hp // n, s // tq),
        in_specs=[spec(tq, 0), spec(s, 1), spec(s, 2)],
        out_specs=pl.BlockSpec((n, None, tq, LANES), lambda bi, hi, qi: (hi, bi, qi, 0)),
        out_shape=jax.ShapeDtypeStruct((hp, b, s, LANES), BF16),
        compiler_params=_cparams(("parallel", "parallel", "arbitrary")),
        name="stickbreak",
    )(planes, planes, planes)


def _pack_pair(x):
    n = x.shape[1] // 2
    lo = lax.bitcast_convert_type(x[:, :n].astype(BF16).astype(F32), jnp.int32)
    hi = lax.bitcast_convert_type(x[:, n:].astype(BF16).astype(F32), jnp.int32)
    return lax.shift_right_logical(lo, 16) | (hi & HIGH_HALF_MASK)


def _unpack_pair(p):
    lo = lax.bitcast_convert_type(lax.shift_left(p, 16), F32)
    hi = lax.bitcast_convert_type(p & HIGH_HALF_MASK, F32)
    return lo, hi


def _route(h, wr, run_ref):
    tm = h.shape[0]
    h_hi = h.astype(BF16)
    h_lo = (h - h_hi.astype(F32)).astype(BF16)
    logits = _dot(h_hi, wr[0]) + (_dot(h_lo, wr[0]) + _dot(h_hi, wr[1]))
    lane = lax.broadcasted_iota(jnp.int32, (1, LANES), 1).astype(F32)
    neg = jnp.float32(-jnp.inf)
    lg = jnp.where(lane < N_EXPERTS, logits, neg)
    m1 = jnp.max(lg, axis=-1, keepdims=True)
    i1 = jnp.min(jnp.where(lg == m1, lane, float(LANES)), axis=-1, keepdims=True)
    sel1 = lane == i1
    lg2 = jnp.where(sel1, neg, lg)
    m2 = jnp.max(lg2, axis=-1, keepdims=True)
    i2 = jnp.min(jnp.where(lg2 == m2, lane, float(LANES)), axis=-1, keepdims=True)
    sel2 = lane == i2
    e2 = jnp.exp(m2 - m1)
    w1 = 1.0 / (1.0 + e2)
    w2 = e2 * w1

    both = jnp.where(jnp.logical_or(sel1, sel2), 1.0, 0.0)
    row = lax.broadcasted_iota(jnp.int32, (tm, tm), 0)
    col = lax.broadcasted_iota(jnp.int32, (tm, tm), 1)
    before = jnp.where(col < row, 1.0, 0.0).astype(BF16)
    cnt = run_ref[...] + _dot(before, both.astype(BF16))
    rank1 = jnp.sum(jnp.where(sel1, cnt, 0.0), axis=-1, keepdims=True)
    rank2 = jnp.sum(jnp.where(sel2, cnt, 0.0), axis=-1, keepdims=True)
    run_ref[...] = run_ref[...] + jnp.sum(both, axis=0, keepdims=True)

    meta = jnp.zeros((tm, LANES), F32)
    for k, v in enumerate((w1, w2, i1, i2, rank1, rank2)):
        meta = jnp.where(lane == float(k), v, meta)
    return meta


def _combine_kernel(*refs, with_router):
    if with_router:
        (x_ref, oa_ref, ob_ref, ga0_ref, ga1_ref, gb0_ref, gb1_ref, pa_ref, pb_ref, wo_ref, g2_ref, wr_ref,
         xo_ref, h2_ref, meta_ref, tot_ref, run_ref) = refs
    else:
        (x_ref, oa_ref, ob_ref, ga0_ref, ga1_ref, gb0_ref, gb1_ref, pa_ref, pb_ref, wo_ref, g2_ref,
         xo_ref, h2_ref) = refs

    def cat(*plane_refs):
        return jnp.concatenate([ref[c] for ref in plane_refs for c in range(ref.shape[0])], axis=1)

    ya = _dot(cat(oa_ref), pa_ref[...])
    yb = _dot(cat(ob_ref), pb_ref[...])
    y = _sigmoid(cat(ga0_ref, ga1_ref).astype(F32)) * ya + _sigmoid(cat(gb0_ref, gb1_ref).astype(F32)) * yb
    xn = x_ref[...] + _dot(y.astype(BF16), wo_ref[...])
    xo_ref[...] = xn
    h2 = _rms(xn, g2_ref[...])
    if with_router:
        @pl.when(pl.program_id(0) == 0)
        def _():
            run_ref[...] = jnp.zeros_like(run_ref)

        h2_ref[...] = _pack_pair(h2)
        meta_ref[...] = _route(h2, wr_ref[...], run_ref)
        tot_ref[...] = jnp.broadcast_to(run_ref[...], tot_ref.shape)
    else:
        h2_ref[...] = h2.astype(BF16)


def _combine(x2, oa, ob, planes, pa, pb, wo, g2, wr, *, gate_plane, tm=512):
    t, d = x2.shape
    tm = min(tm, t)
    na, nb = oa.shape[0], ob.shape[0]
    ng = d // LANES // 2
    with_router = wr is not None
    gblk = gate_plane // ng
    in_specs = [pl.BlockSpec((tm, d), lambda i: (i, 0)),
                pl.BlockSpec((na, tm, LANES), lambda i: (0, i, 0)),
                pl.BlockSpec((nb, tm, LANES), lambda i: (0, i, 0)),
                pl.BlockSpec((ng, tm, LANES), lambda i: (gblk, i, 0)),
                pl.BlockSpec((ng, tm, LANES), lambda i: (gblk + 1, i, 0)),
                pl.BlockSpec((ng, tm, LANES), lambda i: (gblk + 2, i, 0)),
                pl.BlockSpec((ng, tm, LANES), lambda i: (gblk + 3, i, 0)),
                pl.BlockSpec(pa.shape, lambda i: (0, 0)),
                pl.BlockSpec(pb.shape, lambda i: (0, 0)),
                pl.BlockSpec(wo.shape, lambda i: (0, 0)),
                pl.BlockSpec((1, d), lambda i: (0, 0))]
    args = [x2, oa, ob, planes, planes, planes, planes, pa, pb, wo, g2.reshape(1, d)]
    out_specs = [pl.BlockSpec((tm, d), lambda i: (i, 0))]
    out_shape = [jax.ShapeDtypeStruct((t, d), F32)]
    scratch = []
    if with_router:
        in_specs.append(pl.BlockSpec(wr.shape, lambda i: (0, 0, 0)))
        args.append(wr)
        out_specs += [pl.BlockSpec((tm, d // 2), lambda i: (i, 0)),
                      pl.BlockSpec((tm, LANES), lambda i: (i, 0)),
                      pl.BlockSpec((SUBLANES, LANES), lambda i: (0, 0))]
        out_shape += [jax.ShapeDtypeStruct((t, d // 2), jnp.int32),
                      jax.ShapeDtypeStruct((t, LANES), F32),
                      jax.ShapeDtypeStruct((SUBLANES, LANES), F32)]
        scratch = [pltpu.VMEM((1, LANES), F32)]
    else:
        out_specs.append(pl.BlockSpec((tm, d), lambda i: (i, 0)))
        out_shape.append(jax.ShapeDtypeStruct((t, d), BF16))
    return pl.pallas_call(
        functools.partial(_combine_kernel, with_router=with_router),
        grid=(t // tm,),
        in_specs=in_specs, out_specs=out_specs, out_shape=out_shape, scratch_shapes=scratch,
        compiler_params=_cparams(("arbitrary",)),
        name="combine_router" if with_router else "combine",
    )(*args)


def _swiglu_kernel(h_ref, x_ref, wg_ref, wu_ref, wd_ref, o_ref, acc_ref):
    j = pl.program_id(1)

    @pl.when(j == 0)
    def _():
        acc_ref[...] = jnp.zeros_like(acc_ref)

    h = h_ref[...]
    a = _dot(h, wg_ref[...])
    u = _dot(h, wu_ref[...])
    acc_ref[...] += _dot((a * _sigmoid(a) * u).astype(BF16), wd_ref[...])

    @pl.when(j == pl.num_programs(1) - 1)
    def _():
        o_ref[...] = x_ref[...] + acc_ref[...]


def _swiglu(h2, x2, wg, wu, wd, *, tm=1024, tf=FFN_TILE):
    t, d = x2.shape
    f = wg.shape[1]
    tm = min(tm, t)
    return pl.pallas_call(
        _swiglu_kernel,
        grid=(t // tm, f // tf),
        in_specs=[pl.BlockSpec((tm, d), lambda i, j: (i, 0)),
                  pl.BlockSpec((tm, d), lambda i, j: (i, 0)),
                  pl.BlockSpec((d, tf), lambda i, j: (0, j)),
                  pl.BlockSpec((d, tf), lambda i, j: (0, j)),
                  pl.BlockSpec((tf, d), lambda i, j: (j, 0))],
        out_specs=pl.BlockSpec((tm, d), lambda i, j: (i, 0)),
        out_shape=jax.ShapeDtypeStruct((t, d), F32),
        scratch_shapes=[pltpu.VMEM((tm, d), F32)],
        compiler_params=_cparams(("parallel", "arbitrary")),
        name="swiglu",
    )(h2, x2, wg, wu, wd)


def _sc_workers():
    info = plsc.get_sparse_core_info()
    return info.num_cores, info.num_cores * info.num_subcores


def _sc_scatter_rows(rows, pos, n_out):
    t, w = rows.shape
    kk = pos.shape[0]
    nc, nw = _sc_workers()
    per_w = t // nw
    nchunk = per_w // SC_CHUNK
    mesh = plsc.VectorSubcoreMesh(core_axis_name="c", subcore_axis_name="s")

    @functools.partial(
        pl.kernel, mesh=mesh, out_type=jax.ShapeDtypeStruct((n_out, w), rows.dtype),
        scratch_types=[pltpu.VMEM((kk, nchunk, SC_CHUNK), jnp.int32), pltpu.VMEM((SC_BUFFERS, SC_CHUNK, w), rows.dtype),
                       pltpu.SemaphoreType.DMA((SC_BUFFERS,)), pltpu.SemaphoreType.DMA((SC_BUFFERS,))],
        name="moe_dispatch")
    def scatter_kernel(rows_hbm, pos_hbm, out_hbm, pos_v, rows_v, load_sem, store_sem):
        wid = lax.axis_index("s") * nc + lax.axis_index("c")
        base = wid * per_w
        pltpu.sync_copy(pos_hbm.at[wid], pos_v)

        def load(j, b):
            src = rows_hbm.at[pl.ds(pl.multiple_of(base + j * SC_CHUNK, SC_CHUNK), SC_CHUNK)]
            return pltpu.make_async_copy(src, rows_v.at[b], load_sem.at[b])

        def store(j, b, k):
            return pltpu.make_async_copy(rows_v.at[b], out_hbm.at[pos_v.at[k, j]], store_sem.at[b])

        for b in range(min(SC_BUFFERS, nchunk)):
            load(b, b).start()

        @pl.loop(0, nchunk, step=SC_BUFFERS)
        def _(g):
            for b in range(SC_BUFFERS):
                j = g + b

                @pl.when(j < nchunk)
                def _():
                    load(j, b).wait()
                    for k in range(kk):
                        store(j, b, k).start()
                    for k in range(kk):
                        store(j, b, k).wait()

                    @pl.when(j + SC_BUFFERS < nchunk)
                    def _():
                        load(j + SC_BUFFERS, b).start()

    return scatter_kernel(rows, pos.reshape(kk, nw, nchunk, SC_CHUNK).transpose(1, 0, 2, 3))


def _sc_gather_rows(table, idx):
    n = idx.shape[0]
    w = table.shape[1]
    nc, nw = _sc_workers()
    per_w = n // nw
    nchunk = per_w // SC_CHUNK
    mesh = plsc.VectorSubcoreMesh(core_axis_name="c", subcore_axis_name="s")

    @functools.partial(
        pl.kernel, mesh=mesh, out_type=jax.ShapeDtypeStruct((n, w), table.dtype),
        scratch_types=[pltpu.VMEM((nchunk, SC_CHUNK), jnp.int32), pltpu.VMEM((SC_BUFFERS, SC_CHUNK, w), table.dtype),
                       pltpu.SemaphoreType.DMA((SC_BUFFERS,)), pltpu.SemaphoreType.DMA((SC_BUFFERS,))],
        name="moe_collect")
    def gather_kernel(table_hbm, idx_hbm, out_hbm, idx_v, rows_v, gather_sem, store_sem):
        wid = lax.axis_index("s") * nc + lax.axis_index("c")
        base = wid * per_w
        pltpu.sync_copy(idx_hbm.at[wid], idx_v)

        def gather(j, b):
            return pltpu.make_async_copy(table_hbm.at[idx_v.at[j]], rows_v.at[b], gather_sem.at[b])

        def store(j, b):
            dst = out_hbm.at[pl.ds(pl.multiple_of(base + j * SC_CHUNK, SC_CHUNK), SC_CHUNK)]
            return pltpu.make_async_copy(rows_v.at[b], dst, store_sem.at[b])

        for b in range(min(SC_BUFFERS, nchunk)):
            gather(b, b).start()

        @pl.loop(0, nchunk, step=SC_BUFFERS)
        def _(g):
            for b in range(SC_BUFFERS):
                j = g + b

                @pl.when(j < nchunk)
                def _():
                    gather(j, b).wait()
                    store(j, b).start()
                    store(j, b).wait()

                    @pl.when(j + SC_BUFFERS < nchunk)
                    def _():
                        gather(j + SC_BUFFERS, b).start()

    return gather_kernel(table, idx.reshape(nw, nchunk, SC_CHUNK))


def _experts_kernel(te_ref, nu_ref, xs_ref, wg_ref, wu_ref, wd_ref, ys_ref, h_scr, acc_ref):
    i = pl.program_id(0)
    j = pl.program_id(1)

    @pl.when(i < nu_ref[0])
    def _():
        @pl.when(j == 0)
        def _():
            lo, hi = _unpack_pair(xs_ref[...])
            half = lo.shape[1]
            h_scr[:, :half] = lo.astype(BF16)
            h_scr[:, half:] = hi.astype(BF16)
            acc_ref[...] = jnp.zeros_like(acc_ref)

        h = h_scr[...]
        a = _dot(h, wg_ref[...])
        u = _dot(h, wu_ref[...])
        acc_ref[...] += _dot((a * _sigmoid(a) * u).astype(BF16), wd_ref[...])

        @pl.when(j == pl.num_programs(1) - 1)
        def _():
            ys_ref[...] = _pack_pair(acc_ref[...])


def _experts(xs, tile_expert, n_used, wg, wu, wd, *, tm, tf=FFN_TILE):
    r, half = xs.shape
    d = 2 * half
    f = wg.shape[2]
    grid_spec = pltpu.PrefetchScalarGridSpec(
        num_scalar_prefetch=2,
        grid=(r // tm, f // tf),
        in_specs=[pl.BlockSpec((tm, half), lambda i, j, te, nu: (i, 0)),
                  pl.BlockSpec((None, d, tf), lambda i, j, te, nu: (te[i], 0, j)),
                  pl.BlockSpec((None, d, tf), lambda i, j, te, nu: (te[i], 0, j)),
                  pl.BlockSpec((None, tf, d), lambda i, j, te, nu: (te[i], j, 0))],
        out_specs=pl.BlockSpec((tm, half), lambda i, j, te, nu: (i, 0)),
        scratch_shapes=[pltpu.VMEM((tm, d), BF16), pltpu.VMEM((tm, d), F32)])
    return pl.pallas_call(
        _experts_kernel,
        grid_spec=grid_spec,
        out_shape=jax.ShapeDtypeStruct((r, half), jnp.int32),
        compiler_params=_cparams(("arbitrary", "arbitrary")),
        name="moe_experts",
    )(tile_expert, n_used, xs, wg, wu, wd)


def _moe_out_kernel(*refs, final_norm):
    if final_norm:
        x_ref, y1_ref, y2_ref, meta_ref, gf_ref, o_ref = refs
    else:
        x_ref, y1_ref, y2_ref, meta_ref, o_ref = refs
    meta = meta_ref[...]
    w1, w2 = meta[:, 0:1], meta[:, 1:2]
    lo1, hi1 = _unpack_pair(y1_ref[...])
    lo2, hi2 = _unpack_pair(y2_ref[...])
    half = lo1.shape[1]
    x = x_ref[...]
    y = jnp.concatenate([x[:, :half] + w1 * lo1 + w2 * lo2, x[:, half:] + w1 * hi1 + w2 * hi2], axis=1)
    if final_norm:
        y = _rms(y, gf_ref[...])
    o_ref[...] = y


def _moe_out(x2, yg, meta, gf, *, tm=512):
    t, d = x2.shape
    tm = min(tm, t)
    nblk = t // tm
    final_norm = gf is not None
    in_specs = [pl.BlockSpec((tm, d), lambda i: (i, 0)),
                pl.BlockSpec((tm, d // 2), lambda i: (i, 0)),
                pl.BlockSpec((tm, d // 2), lambda i: (i + nblk, 0)),
                pl.BlockSpec((tm, LANES), lambda i: (i, 0))]
    args = [x2, yg, yg, meta]
    if final_norm:
        in_specs.append(pl.BlockSpec((1, d), lambda i: (0, 0)))
        args.append(gf.reshape(1, d))
    return pl.pallas_call(
        functools.partial(_moe_out_kernel, final_norm=final_norm),
        grid=(nblk,),
        in_specs=in_specs,
        out_specs=pl.BlockSpec((tm, d), lambda i: (i, 0)),
        out_shape=jax.ShapeDtypeStruct((t, d), F32),
        compiler_params=_cparams(("parallel",)),
        name="moe_out",
    )(*args)


def _moe(h2p, xn, meta, totals, wg, wu, wd, gf, *, tm=MOE_TILE):
    t = xn.shape[0]
    tm = min(tm, t)
    ne = wg.shape[0]
    cnt = totals[0, :ne].astype(jnp.int32)
    cap = ((cnt + tm - 1) // tm) * tm
    ends = jnp.cumsum(cap)
    off = ends - cap
    n_tiles = 2 * t // tm + ne
    tile_start = jnp.arange(n_tiles, dtype=jnp.int32) * tm
    tile_expert = jnp.minimum(jnp.sum(tile_start[:, None] >= ends[None, :], axis=1), ne - 1).astype(jnp.int32)
    n_used = (ends[-1:] // tm).astype(jnp.int32)
    e12 = meta[:, 2:4].astype(jnp.int32)
    pos = (off[e12] + meta[:, 4:6].astype(jnp.int32)).T
    xs = _sc_scatter_rows(h2p, pos, n_tiles * tm)
    ys = _experts(xs, tile_expert, n_used, wg, wu, wd, tm=tm)
    yg = _sc_gather_rows(ys, pos.reshape(2 * t))
    return _moe_out(xn, yg, meta, gf)


def _final_norm_kernel(x_ref, g_ref, o_ref):
    o_ref[...] = _rms(x_ref[...], g_ref[...])


def _final_norm(x2, g, *, tm=512):
    t, d = x2.shape
    tm = min(tm, t)
    return pl.pallas_call(
        _final_norm_kernel,
        grid=(t // tm,),
        in_specs=[pl.BlockSpec((tm, d), lambda i: (i, 0)), pl.BlockSpec((1, d), lambda i: (0, 0))],
        out_specs=pl.BlockSpec((tm, d), lambda i: (i, 0)),
        out_shape=jax.ShapeDtypeStruct((t, d), F32),
        compiler_params=_cparams(("parallel",)),
        name="final_norm",
    )(x2, g.reshape(1, d))


def kernel(x, mix_norm, w_in, hgrn_lb_logits, hgrn_out_norm, w_branch_hgrn, w_branch_sb, w_out, ffn_norm,
           dense_w_gate, dense_w_up, dense_w_down, moe_router, moe_w_gate, moe_w_up, moe_w_down, final_norm):
    b, s, d = x.shape
    t = b * s
    depth = w_in.shape[0]
    hgrn_width = HGRN_HEADS * HGRN_HEAD_DIM
    sb_width = SB_HEADS * SB_HEAD_DIM
    hgrn_plane = 0
    sb_plane = hgrn_plane + 4 * hgrn_width // LANES
    gate_plane = sb_plane + 3 * sb_width // LANES

    lb_all = jnp.cumsum(jax.nn.softmax(hgrn_lb_logits.astype(F32), axis=0), axis=0)
    lb_all = lb_all - lb_all[0:1]
    lb_params = jnp.stack([jnp.log(lb_all), jnp.log1p(-lb_all), 1.0 - lb_all], axis=1)
    lb_params = lb_params.reshape(depth, 3, HGRN_HEADS, HGRN_HEAD_DIM).transpose(0, 2, 1, 3)

    x2 = x.reshape(t, d)
    for layer in range(depth):
        planes = _inproj(x2, mix_norm[layer], w_in[layer].astype(BF16))
        planes4 = planes.reshape(planes.shape[0], b, s, LANES)
        oa = _hgrn(planes4, lb_params[layer], hgrn_out_norm[layer], first_plane=hgrn_plane)
        ob = _sb_attention(planes4, first_plane=sb_plane)
        j = layer // 2
        moe = layer % 2 == 1
        wr = None
        if moe:
            wr = jnp.zeros((d, LANES), F32).at[:, :N_EXPERTS].set(moe_router[j].astype(F32))
            wr_hi = wr.astype(BF16)
            wr = jnp.stack([wr_hi, (wr - wr_hi.astype(F32)).astype(BF16)])
        res = _combine(x2, oa.reshape(HGRN_HEADS, t, LANES), ob.reshape(ob.shape[0], t, LANES), planes,
                       w_branch_hgrn[layer].astype(BF16), w_branch_sb[layer].astype(BF16),
                       w_out[layer].astype(BF16), ffn_norm[layer], wr, gate_plane=gate_plane)
        last = layer == depth - 1
        if moe:
            xn, h2p, meta, totals = res
            x2 = _moe(h2p, xn, meta, totals, moe_w_gate[j].astype(BF16), moe_w_up[j].astype(BF16),
                      moe_w_down[j].astype(BF16), final_norm if last else None)
        else:
            xn, h2 = res
            x2 = _swiglu(h2, xn, dense_w_gate[j].astype(BF16), dense_w_up[j].astype(BF16),
                         dense_w_down[j].astype(BF16))
            if last:
                x2 = _final_norm(x2, final_norm)
    return x2.reshape(b, s, d)
```

```python
import functools

import jax
import jax.numpy as jnp
from jax import lax
from jax.experimental import pallas as pl
from jax.experimental.pallas import tpu as pltpu
from jax.experimental.pallas import tpu_sc as plsc

F32 = jnp.float32
BF16 = jnp.bfloat16

LANES = 128
SUBLANES = 8
HGRN_HEADS = 4
HGRN_HEAD_DIM = 128
SB_HEADS = 8
SB_HEAD_DIM = 64
SB_HEADS_PER_PLANE = LANES // SB_HEAD_DIM
N_EXPERTS = 8
RMS_EPS = 1e-6
VMEM_LIMIT_BYTES = 56 * 1024 * 1024

HGRN_CHUNK = 128
HGRN_DIAG = 8
HGRN_ROWS = 512
HGRN_HEADS_PER_STEP = 4
SB_BLOCK = 256
FFN_TILE = 512
MOE_TILE = 1024
SC_CHUNK = 64
SC_BUFFERS = 3
HIGH_HALF_MASK = -65536
SB_PLANES = 2
LOG2_E = 1.4426950408889634
SB_UNDERFLOW_LOG = -105.0


def _cparams(semantics):
    return pltpu.CompilerParams(dimension_semantics=semantics, vmem_limit_bytes=VMEM_LIMIT_BYTES)


def _rms(x, g):
    ms = jnp.mean(x * x, axis=-1, keepdims=True)
    return x * lax.rsqrt(ms + RMS_EPS) * g


def _sigmoid(x):
    return 0.5 * jnp.tanh(0.5 * x) + 0.5


def _dot(a, b):
    return jnp.dot(a, b, preferred_element_type=F32)


def _dot_nt(a, b):
    return lax.dot_general(a, b, (((1,), (1,)), ((), ())), preferred_element_type=F32)


def _dot_tn(a, b):
    return lax.dot_general(a, b, (((0,), (0,)), ((), ())), preferred_element_type=F32)


def _inproj_kernel(x_ref, g_ref, w_ref, o_ref, *, tn):
    h = _rms(x_ref[...], g_ref[...]).astype(BF16)
    for j in range(w_ref.shape[1] // tn):
        r = _dot(h, w_ref[:, j * tn:(j + 1) * tn])
        for c in range(tn // LANES):
            o_ref[j * (tn // LANES) + c] = r[:, c * LANES:(c + 1) * LANES].astype(BF16)


def _inproj(x2, g, w, layer, *, tm=512, tn=512):
    t, d = x2.shape
    n = w.shape[2]
    tm = min(tm, t)
    return pl.pallas_call(
        functools.partial(_inproj_kernel, tn=tn),
        grid=(t // tm,),
        in_specs=[pl.BlockSpec((tm, d), lambda i: (i, 0)),
                  pl.BlockSpec((1, d), lambda i: (0, 0)),
                  pl.BlockSpec((None, d, n), lambda i: (layer, 0, 0))],
        out_specs=pl.BlockSpec((n // LANES, tm, LANES), lambda i: (0, i, 0)),
        out_shape=jax.ShapeDtypeStruct((n // LANES, t, LANES), BF16),
        compiler_params=_cparams(("parallel",)),
        name="inproj",
    )(x2, g.reshape(1, d), w)


def _hgrn_chunk(zq, zf, vi, zog, lbp, gn, st):
    c = zq.shape[0]
    log_lb, log1m_lb, one_m_lb = lbp[0:1], lbp[1:2], lbp[2:3]

    q = zq * _sigmoid(zq)
    log_sig = jnp.minimum(zf, 0.0) - jnp.log(1.0 + jnp.exp(-jnp.abs(zf)))
    k = one_m_lb * _sigmoid(-zf)
    cc = log1m_lb + log_sig
    log_f = jnp.maximum(log_lb, cc) + jnp.log(1.0 + jnp.exp(-jnp.abs(log_lb - cc)))
    log_f = log_f * LOG2_E
    yield

    hi = log_f.astype(BF16)
    r1 = log_f - hi.astype(F32)
    mid = r1.astype(BF16)
    lo = (r1 - mid.astype(F32)).astype(BF16)
    row = lax.broadcasted_iota(jnp.int32, (c, c), 0)
    col = lax.broadcasted_iota(jnp.int32, (c, c), 1)
    tri = jnp.where(col <= row, 1.0, 0.0).astype(BF16)
    parts = _dot(tri, jnp.concatenate([hi, mid, lo], axis=1))
    b = parts[:, :LANES] + parts[:, LANES:2 * LANES] + parts[:, 2 * LANES:]
    b_prev = b - log_f
    b_last = b[c - 1:c, :]
    yield

    o = _dot_nt((q * jnp.exp2(b)).astype(BF16), st.astype(BF16))
    k_tail = (k * jnp.exp2(b_last - b)).astype(BF16)
    st_new = st * jnp.exp2(b_last) + _dot_tn(vi, k_tail)
    yield

    rows = lax.broadcasted_iota(jnp.int32, (c, 1), 0)
    scores = jnp.zeros((c, c), F32)
    m = HGRN_DIAG
    while m < c:
        n = c // m
        base = b_prev.reshape(n, m, LANES)[:, 0:1, :]
        p = b.reshape(n, m, LANES) - base
        suf = (p[:, m - 1:m, :] - p).reshape(c, LANES)
        p = p.reshape(c, LANES)
        odd = ((rows // m) % 2) == 1
        q_m = jnp.where(odd, q * jnp.exp2(p), 0.0).astype(BF16)
        k_m = jnp.where(odd, 0.0, k * jnp.exp2(suf)).astype(BF16)
        s_m = _dot_nt(q_m, k_m)
        scores = scores + jnp.where((row // (2 * m)) == (col // (2 * m)), s_m, 0.0)
        m *= 2
        yield
    blocks = (c // HGRN_DIAG, HGRN_DIAG, LANES)
    q3, k3, b3 = (a.reshape(blocks) for a in (q, k, b))
    sub_diagonal = jnp.where((row // HGRN_DIAG) == (col // HGRN_DIAG), row - col, -1)
    for j in range(HGRN_DIAG):
        ks, bs = (k3, b3) if j == 0 else (pltpu.roll(k3, j, 1), pltpu.roll(b3, j, 1))
        s = jnp.sum(q3 * ks * jnp.exp2(b3 - bs), axis=-1, keepdims=True).reshape(c, 1)
        scores = jnp.where(sub_diagonal == j, s, scores)
        yield
    o = o + _dot(scores.astype(BF16), vi)

    out = _rms(o, gn) * (zog * _sigmoid(zog))
    return out.astype(BF16), st_new


def _hgrn_kernel(q_ref, f_ref, i_ref, og_ref, lb_ref, gn_ref, o_ref, st_ref):
    @pl.when(pl.program_id(2) == 0)
    def _():
        st_ref[...] = jnp.zeros_like(st_ref)

    gn = gn_ref[...]

    def body(i, carry):
        rows = pl.ds(pl.multiple_of(i * HGRN_CHUNK, HGRN_CHUNK), HGRN_CHUNK)
        heads = [_hgrn_chunk(q_ref[h, rows, :].astype(F32), f_ref[h, rows, :].astype(F32), i_ref[h, rows, :],
                             og_ref[h, rows, :].astype(F32), lb_ref[h], gn, st_ref[h])
                 for h in range(q_ref.shape[0])]
        pending = dict(enumerate(heads))
        while pending:
            for h, gen in list(pending.items()):
                try:
                    next(gen)
                except StopIteration as done:
                    out, st_new = done.value
                    o_ref[h, rows, :] = out
                    st_ref[h] = st_new
                    del pending[h]
        return carry

    lax.fori_loop(0, q_ref.shape[1] // HGRN_CHUNK, body, 0)


def _hgrn(planes, lb_params, gn, *, first_plane):
    _, b, s, _ = planes.shape
    rows = min(HGRN_ROWS, s)
    h = HGRN_HEADS
    n = HGRN_HEADS_PER_STEP

    def plane_spec(k):
        blk = (first_plane + k * h) // n
        return pl.BlockSpec((n, None, rows, LANES), lambda bi, hi, ci: (blk + hi, bi, ci, 0))

    return pl.pallas_call(
        _hgrn_kernel,
        grid=(b, h // n, s // rows),
        in_specs=[plane_spec(0), plane_spec(1), plane_spec(2), plane_spec(3),
                  pl.BlockSpec((n, 3, LANES), lambda bi, hi, ci: (hi, 0, 0)),
                  pl.BlockSpec((1, LANES), lambda bi, hi, ci: (0, 0))],
        out_specs=pl.BlockSpec((n, None, rows, LANES), lambda bi, hi, ci: (hi, bi, ci, 0)),
        out_shape=jax.ShapeDtypeStruct((h, b, s, LANES), BF16),
        scratch_shapes=[pltpu.VMEM((n, HGRN_HEAD_DIM, HGRN_HEAD_DIM), F32)],
        compiler_params=_cparams(("parallel", "parallel", "arbitrary")),
        name="hgrn2",
    )(planes, planes, planes, planes, lb_params, gn.reshape(1, LANES))


def _sb_kernel(q_ref, k_ref, v_ref, o_ref):
    n_planes, tq, _ = q_ref.shape
    tk = tq
    qi = pl.program_id(2)
    lane = lax.broadcasted_iota(jnp.int32, (1, LANES), 1)
    first = lane < SB_HEAD_DIM
    q_heads = []
    for p in range(n_planes):
        q2 = q_ref[p] * jnp.asarray(SB_HEAD_DIM ** -0.5, BF16)
        q_heads += [(p, jnp.where(first, q2, jnp.zeros_like(q2))), (p, jnp.where(first, jnp.zeros_like(q2), q2))]
    n_heads = len(q_heads)
    strict = lax.broadcasted_iota(jnp.int32, (tq, tk), 1) < lax.broadcasted_iota(jnp.int32, (tq, tk), 0)
    later = jnp.where(lax.broadcasted_iota(jnp.int32, (tk, tk), 0) > lax.broadcasted_iota(jnp.int32, (tk, tk), 1),
                      1.0, 0.0).astype(BF16)

    def block(kb, accs, sums, diagonal):
        rows = pl.ds(pl.multiple_of(kb * tk, tk), tk)
        zs = [_dot_nt(qh, k_ref[p, rows, :]) for p, qh in q_heads]
        log_keeps = []
        for z in zs:
            neg_z = -z
            log_keep = jnp.minimum(neg_z, 0.0) - jnp.log(1.0 + jnp.exp(jnp.minimum(z, neg_z)))
            if diagonal:
                log_keep = jnp.where(strict, log_keep, 0.0)
            log_keeps.append(log_keep)
        afters = [_dot(lk.astype(BF16), later) for lk in log_keeps]
        ws = []
        for h in range(n_heads):
            w = jnp.exp(zs[h] + log_keeps[h] + afters[h] + sums[h])
            if diagonal:
                w = jnp.where(strict, w, 0.0)
            ws.append(w.astype(BF16))
        new_accs = [accs[h] + _dot(ws[h], v_ref[q_heads[h][0], rows, :]) for h in range(n_heads)]
        new_sums = [sums[h] + jnp.sum(log_keeps[h], axis=-1, keepdims=True) for h in range(n_heads)]
        top = new_sums[0]
        for h in range(1, n_heads):
            top = jnp.maximum(top, new_sums[h])
        return tuple(new_accs), tuple(new_sums), (jnp.max(top) < SB_UNDERFLOW_LOG).astype(jnp.int32)

    zero_acc = jnp.zeros((tq, LANES), F32)
    zero_sum = jnp.zeros((tq, 1), F32)
    accs, sums, done = block(qi, (zero_acc,) * n_heads, (zero_sum,) * n_heads, True)

    def cond(carry):
        it, done, _, _ = carry
        return jnp.logical_and(it <= qi, done == 0)

    def body(carry):
        it, _, accs, sums = carry
        accs, sums, done = block(qi - it, accs, sums, False)
        return it + 1, done, accs, sums

    _, _, accs, _ = lax.while_loop(cond, body, (jnp.int32(1), done, accs, sums))
    for p in range(n_planes):
        o_ref[p] = jnp.where(first, accs[2 * p], accs[2 * p + 1]).astype(BF16)


def _sb_attention(planes, *, first_plane):
    _, b, s, _ = planes.shape
    hp = SB_HEADS // SB_HEADS_PER_PLANE
    tq = SB_BLOCK
    n = SB_PLANES

    def spec(rows, k):
        blk = (first_plane + k * hp) // n
        return pl.BlockSpec((n, None, rows, LANES), lambda bi, hi, qi: (blk + hi, bi, qi if rows == tq else 0, 0))

    return pl.pallas_call(
        _sb_kernel,
        grid=(b, hp // n, s // tq),
        in_specs=[spec(tq, 0), spec(s, 1), spec(s, 2)],
        out_specs=pl.BlockSpec((n, None, tq, LANES), lambda bi, hi, qi: (hi, bi, qi, 0)),
        out_shape=jax.ShapeDtypeStruct((hp, b, s, LANES), BF16),
        compiler_params=_cparams(("parallel", "parallel", "arbitrary")),
        name="stickbreak",
    )(planes, planes, planes)


def _pack_pair(x):
    n = x.shape[1] // 2
    lo = lax.bitcast_convert_type(x[:, :n].astype(BF16).astype(F32), jnp.int32)
    hi = lax.bitcast_convert_type(x[:, n:].astype(BF16).astype(F32), jnp.int32)
    return lax.shift_right_logical(lo, 16) | (hi & HIGH_HALF_MASK)


def _unpack_pair(p):
    lo = lax.bitcast_convert_type(lax.shift_left(p, 16), F32)
    hi = lax.bitcast_convert_type(p & HIGH_HALF_MASK, F32)
    return lo, hi


def _route(h, wr, run_ref):
    tm = h.shape[0]
    h_hi = h.astype(BF16)
    h_lo = (h - h_hi.astype(F32)).astype(BF16)
    logits = _dot(h_hi, wr[0]) + (_dot(h_lo, wr[0]) + _dot(h_hi, wr[1]))
    lane = lax.broadcasted_iota(jnp.int32, (1, LANES), 1).astype(F32)
    neg = jnp.float32(-jnp.inf)
    lg = jnp.where(lane < N_EXPERTS, logits, neg)
    m1 = jnp.max(lg, axis=-1, keepdims=True)
    i1 = jnp.min(jnp.where(lg == m1, lane, float(LANES)), axis=-1, keepdims=True)
    sel1 = lane == i1
    lg2 = jnp.where(sel1, neg, lg)
    m2 = jnp.max(lg2, axis=-1, keepdims=True)
    i2 = jnp.min(jnp.where(lg2 == m2, lane, float(LANES)), axis=-1, keepdims=True)
    sel2 = lane == i2
    e2 = jnp.exp(m2 - m1)
    w1 = 1.0 / (1.0 + e2)
    w2 = e2 * w1

    both = jnp.where(jnp.logical_or(sel1, sel2), 1.0, 0.0)
    row = lax.broadcasted_iota(jnp.int32, (tm, tm), 0)
    col = lax.broadcasted_iota(jnp.int32, (tm, tm), 1)
    before = jnp.where(col < row, 1.0, 0.0).astype(BF16)
    cnt = run_ref[...] + _dot(before, both.astype(BF16))
    rank1 = jnp.sum(jnp.where(sel1, cnt, 0.0), axis=-1, keepdims=True)
    rank2 = jnp.sum(jnp.where(sel2, cnt, 0.0), axis=-1, keepdims=True)
    run_ref[...] = run_ref[...] + jnp.sum(both, axis=0, keepdims=True)

    meta = jnp.zeros((tm, LANES), F32)
    for k, v in enumerate((w1, w2, i1, i2, rank1, rank2)):
        meta = jnp.where(lane == float(k), v, meta)
    return meta


def _combine_kernel(*refs, with_router):
    if with_router:
        (x_ref, oa_ref, ob_ref, ga0_ref, ga1_ref, gb0_ref, gb1_ref, pa_ref, pb_ref, wo_ref, g2_ref, wr_ref,
         xo_ref, h2_ref, meta_ref, tot_ref, run_ref) = refs
    else:
        (x_ref, oa_ref, ob_ref, ga0_ref, ga1_ref, gb0_ref, gb1_ref, pa_ref, pb_ref, wo_ref, g2_ref,
         xo_ref, h2_ref) = refs

    def cat(*plane_refs):
        return jnp.concatenate([ref[c] for ref in plane_refs for c in range(ref.shape[0])], axis=1)

    ya = _dot(cat(oa_ref), pa_ref[...])
    yb = _dot(cat(ob_ref), pb_ref[...])
    y = _sigmoid(cat(ga0_ref, ga1_ref).astype(F32)) * ya + _sigmoid(cat(gb0_ref, gb1_ref).astype(F32)) * yb
    xn = x_ref[...] + _dot(y.astype(BF16), wo_ref[...])
    xo_ref[...] = xn
    h2 = _rms(xn, g2_ref[...])
    if with_router:
        @pl.when(pl.program_id(0) == 0)
        def _():
            run_ref[...] = jnp.zeros_like(run_ref)

        h2_ref[...] = _pack_pair(h2)
        meta_ref[...] = _route(h2, wr_ref[...], run_ref)
        tot_ref[...] = jnp.broadcast_to(run_ref[...], tot_ref.shape)
    else:
        h2_ref[...] = h2.astype(BF16)


def _combine(x2, oa, ob, planes, pa, pb, wo, layer, g2, wr, *, gate_plane, tm=512):
    t, d = x2.shape
    tm = min(tm, t)
    na, nb = oa.shape[0], ob.shape[0]
    ng = d // LANES // 2
    with_router = wr is not None
    gblk = gate_plane // ng
    in_specs = [pl.BlockSpec((tm, d), lambda i: (i, 0)),
                pl.BlockSpec((na, tm, LANES), lambda i: (0, i, 0)),
                pl.BlockSpec((nb, tm, LANES), lambda i: (0, i, 0)),
                pl.BlockSpec((ng, tm, LANES), lambda i: (gblk, i, 0)),
                pl.BlockSpec((ng, tm, LANES), lambda i: (gblk + 1, i, 0)),
                pl.BlockSpec((ng, tm, LANES), lambda i: (gblk + 2, i, 0)),
                pl.BlockSpec((ng, tm, LANES), lambda i: (gblk + 3, i, 0)),
                pl.BlockSpec((None,) + pa.shape[1:], lambda i: (layer, 0, 0)),
                pl.BlockSpec((None,) + pb.shape[1:], lambda i: (layer, 0, 0)),
                pl.BlockSpec((None,) + wo.shape[1:], lambda i: (layer, 0, 0)),
                pl.BlockSpec((1, d), lambda i: (0, 0))]
    args = [x2, oa, ob, planes, planes, planes, planes, pa, pb, wo, g2.reshape(1, d)]
    out_specs = [pl.BlockSpec((tm, d), lambda i: (i, 0))]
    out_shape = [jax.ShapeDtypeStruct((t, d), F32)]
    scratch = []
    if with_router:
        in_specs.append(pl.BlockSpec(wr.shape, lambda i: (0, 0, 0)))
        args.append(wr)
        out_specs += [pl.BlockSpec((tm, d // 2), lambda i: (i, 0)),
                      pl.BlockSpec((tm, LANES), lambda i: (i, 0)),
                      pl.BlockSpec((SUBLANES, LANES), lambda i: (0, 0))]
        out_shape += [jax.ShapeDtypeStruct((t, d // 2), jnp.int32),
                      jax.ShapeDtypeStruct((t, LANES), F32),
                      jax.ShapeDtypeStruct((SUBLANES, LANES), F32)]
        scratch = [pltpu.VMEM((1, LANES), F32)]
    else:
        out_specs.append(pl.BlockSpec((tm, d), lambda i: (i, 0)))
        out_shape.append(jax.ShapeDtypeStruct((t, d), BF16))
    return pl.pallas_call(
        functools.partial(_combine_kernel, with_router=with_router),
        grid=(t // tm,),
        in_specs=in_specs, out_specs=out_specs, out_shape=out_shape, scratch_shapes=scratch,
        compiler_params=_cparams(("arbitrary",)),
        name="combine_router" if with_router else "combine",
    )(*args)


def _swiglu_kernel(h_ref, x_ref, wg_ref, wu_ref, wd_ref, o_ref, acc_ref):
    j = pl.program_id(1)

    @pl.when(j == 0)
    def _():
        acc_ref[...] = jnp.zeros_like(acc_ref)

    h = h_ref[...]
    a = _dot(h, wg_ref[...])
    u = _dot(h, wu_ref[...])
    acc_ref[...] += _dot((a * _sigmoid(a) * u).astype(BF16), wd_ref[...])

    @pl.when(j == pl.num_programs(1) - 1)
    def _():
        o_ref[...] = x_ref[...] + acc_ref[...]


def _swiglu(h2, x2, wg, wu, wd, layer, *, tm=1024, tf=FFN_TILE):
    t, d = x2.shape
    f = wg.shape[2]
    tm = min(tm, t)
    return pl.pallas_call(
        _swiglu_kernel,
        grid=(t // tm, f // tf),
        in_specs=[pl.BlockSpec((tm, d), lambda i, j: (i, 0)),
                  pl.BlockSpec((tm, d), lambda i, j: (i, 0)),
                  pl.BlockSpec((None, d, tf), lambda i, j: (layer, 0, j)),
                  pl.BlockSpec((None, d, tf), lambda i, j: (layer, 0, j)),
                  pl.BlockSpec((None, tf, d), lambda i, j: (layer, j, 0))],
        out_specs=pl.BlockSpec((tm, d), lambda i, j: (i, 0)),
        out_shape=jax.ShapeDtypeStruct((t, d), F32),
        scratch_shapes=[pltpu.VMEM((tm, d), F32)],
        compiler_params=_cparams(("parallel", "arbitrary")),
        name="swiglu",
    )(h2, x2, wg, wu, wd)


def _sc_workers():
    info = plsc.get_sparse_core_info()
    return info.num_cores, info.num_cores * info.num_subcores


def _sc_scatter_rows(rows, pos, n_out):
    t, w = rows.shape
    kk = pos.shape[0]
    nc, nw = _sc_workers()
    per_w = t // nw
    nchunk = per_w // SC_CHUNK
    mesh = plsc.VectorSubcoreMesh(core_axis_name="c", subcore_axis_name="s")

    @functools.partial(
        pl.kernel, mesh=mesh, out_type=jax.ShapeDtypeStruct((n_out, w), rows.dtype),
        scratch_types=[pltpu.VMEM((kk, nchunk, SC_CHUNK), jnp.int32), pltpu.VMEM((SC_BUFFERS, SC_CHUNK, w), rows.dtype),
                       pltpu.SemaphoreType.DMA((SC_BUFFERS,)), pltpu.SemaphoreType.DMA((SC_BUFFERS,))],
        name="moe_dispatch")
    def scatter_kernel(rows_hbm, pos_hbm, out_hbm, pos_v, rows_v, load_sem, store_sem):
        wid = lax.axis_index("s") * nc + lax.axis_index("c")
        base = wid * per_w
        pltpu.sync_copy(pos_hbm.at[wid], pos_v)

        def load(j, b):
            src = rows_hbm.at[pl.ds(pl.multiple_of(base + j * SC_CHUNK, SC_CHUNK), SC_CHUNK)]
            return pltpu.make_async_copy(src, rows_v.at[b], load_sem.at[b])

        def store(j, b, k):
            return pltpu.make_async_copy(rows_v.at[b], out_hbm.at[pos_v.at[k, j]], store_sem.at[b])

        for b in range(min(SC_BUFFERS, nchunk)):
            load(b, b).start()

        @pl.loop(0, nchunk, step=SC_BUFFERS)
        def _(g):
            for b in range(SC_BUFFERS):
                j = g + b

                @pl.when(j < nchunk)
                def _():
                    load(j, b).wait()
                    for k in range(kk):
                        store(j, b, k).start()
                    for k in range(kk):
                        store(j, b, k).wait()

                    @pl.when(j + SC_BUFFERS < nchunk)
                    def _():
                        load(j + SC_BUFFERS, b).start()

    return scatter_kernel(rows, pos.reshape(kk, nw, nchunk, SC_CHUNK).transpose(1, 0, 2, 3))


def _sc_gather_rows(table, idx):
    n = idx.shape[0]
    w = table.shape[1]
    nc, nw = _sc_workers()
    per_w = n // nw
    nchunk = per_w // SC_CHUNK
    mesh = plsc.VectorSubcoreMesh(core_axis_name="c", subcore_axis_name="s")

    @functools.partial(
        pl.kernel, mesh=mesh, out_type=jax.ShapeDtypeStruct((n, w), table.dtype),
        scratch_types=[pltpu.VMEM((nchunk, SC_CHUNK), jnp.int32), pltpu.VMEM((SC_BUFFERS, SC_CHUNK, w), table.dtype),
                       pltpu.SemaphoreType.DMA((SC_BUFFERS,)), pltpu.SemaphoreType.DMA((SC_BUFFERS,))],
        name="moe_collect")
    def gather_kernel(table_hbm, idx_hbm, out_hbm, idx_v, rows_v, gather_sem, store_sem):
        wid = lax.axis_index("s") * nc + lax.axis_index("c")
        base = wid * per_w
        pltpu.sync_copy(idx_hbm.at[wid], idx_v)

        def gather(j, b):
            return pltpu.make_async_copy(table_hbm.at[idx_v.at[j]], rows_v.at[b], gather_sem.at[b])

        def store(j, b):
            dst = out_hbm.at[pl.ds(pl.multiple_of(base + j * SC_CHUNK, SC_CHUNK), SC_CHUNK)]
            return pltpu.make_async_copy(rows_v.at[b], dst, store_sem.at[b])

        for b in range(min(SC_BUFFERS, nchunk)):
            gather(b, b).start()

        @pl.loop(0, nchunk, step=SC_BUFFERS)
        def _(g):
            for b in range(SC_BUFFERS):
                j = g + b

                @pl.when(j < nchunk)
                def _():
                    gather(j, b).wait()
                    store(j, b).start()
                    store(j, b).wait()

                    @pl.when(j + SC_BUFFERS < nchunk)
                    def _():
                        gather(j + SC_BUFFERS, b).start()

    return gather_kernel(table, idx.reshape(nw, nchunk, SC_CHUNK))


def _experts_kernel(te_ref, nu_ref, xs_ref, wg_ref, wu_ref, wd_ref, ys_ref, h_scr, acc_ref):
    i = pl.program_id(0)
    j = pl.program_id(1)

    @pl.when(i < nu_ref[0])
    def _():
        @pl.when(j == 0)
        def _():
            lo, hi = _unpack_pair(xs_ref[...])
            half = lo.shape[1]
            h_scr[:, :half] = lo.astype(BF16)
            h_scr[:, half:] = hi.astype(BF16)
            acc_ref[...] = jnp.zeros_like(acc_ref)

        h = h_scr[...]
        a = _dot(h, wg_ref[...].astype(BF16))
        u = _dot(h, wu_ref[...].astype(BF16))
        acc_ref[...] += _dot((a * _sigmoid(a) * u).astype(BF16), wd_ref[...].astype(BF16))

        @pl.when(j == pl.num_programs(1) - 1)
        def _():
            ys_ref[...] = _pack_pair(acc_ref[...])


def _experts(xs, tile_expert, n_used, wg, wu, wd, layer, *, tm, tf=FFN_TILE):
    r, half = xs.shape
    d = 2 * half
    f = wg.shape[3]
    grid_spec = pltpu.PrefetchScalarGridSpec(
        num_scalar_prefetch=2,
        grid=(r // tm, f // tf),
        in_specs=[pl.BlockSpec((tm, half), lambda i, j, te, nu: (i, 0)),
                  pl.BlockSpec((None, None, d, tf), lambda i, j, te, nu: (layer, te[i], 0, j)),
                  pl.BlockSpec((None, None, d, tf), lambda i, j, te, nu: (layer, te[i], 0, j)),
                  pl.BlockSpec((None, None, tf, d), lambda i, j, te, nu: (layer, te[i], j, 0))],
        out_specs=pl.BlockSpec((tm, half), lambda i, j, te, nu: (i, 0)),
        scratch_shapes=[pltpu.VMEM((tm, d), BF16), pltpu.VMEM((tm, d), F32)])
    return pl.pallas_call(
        _experts_kernel,
        grid_spec=grid_spec,
        out_shape=jax.ShapeDtypeStruct((r, half), jnp.int32),
        compiler_params=_cparams(("arbitrary", "arbitrary")),
        name="moe_experts",
    )(tile_expert, n_used, xs, wg, wu, wd)


def _moe_out_kernel(*refs, final_norm):
    if final_norm:
        x_ref, y1_ref, y2_ref, meta_ref, gf_ref, o_ref = refs
    else:
        x_ref, y1_ref, y2_ref, meta_ref, o_ref = refs
    meta = meta_ref[...]
    w1, w2 = meta[:, 0:1], meta[:, 1:2]
    lo1, hi1 = _unpack_pair(y1_ref[...])
    lo2, hi2 = _unpack_pair(y2_ref[...])
    half = lo1.shape[1]
    x = x_ref[...]
    y = jnp.concatenate([x[:, :half] + w1 * lo1 + w2 * lo2, x[:, half:] + w1 * hi1 + w2 * hi2], axis=1)
    if final_norm:
        y = _rms(y, gf_ref[...])
    o_ref[...] = y


def _moe_out(x2, yg, meta, gf, *, tm=512):
    t, d = x2.shape
    tm = min(tm, t)
    nblk = t // tm
    final_norm = gf is not None
    in_specs = [pl.BlockSpec((tm, d), lambda i: (i, 0)),
                pl.BlockSpec((tm, d // 2), lambda i: (i, 0)),
                pl.BlockSpec((tm, d // 2), lambda i: (i + nblk, 0)),
                pl.BlockSpec((tm, LANES), lambda i: (i, 0))]
    args = [x2, yg, yg, meta]
    if final_norm:
        in_specs.append(pl.BlockSpec((1, d), lambda i: (0, 0)))
        args.append(gf.reshape(1, d))
    return pl.pallas_call(
        functools.partial(_moe_out_kernel, final_norm=final_norm),
        grid=(nblk,),
        in_specs=in_specs,
        out_specs=pl.BlockSpec((tm, d), lambda i: (i, 0)),
        out_shape=jax.ShapeDtypeStruct((t, d), F32),
        compiler_params=_cparams(("parallel",)),
        name="moe_out",
    )(*args)


def _moe(h2p, xn, meta, totals, wg, wu, wd, layer, gf, *, tm=MOE_TILE):
    t = xn.shape[0]
    tm = min(tm, t)
    ne = wg.shape[1]
    cnt = totals[0, :ne].astype(jnp.int32)
    cap = ((cnt + tm - 1) // tm) * tm
    ends = jnp.cumsum(cap)
    off = ends - cap
    n_tiles = 2 * t // tm + ne
    tile_start = jnp.arange(n_tiles, dtype=jnp.int32) * tm
    tile_expert = jnp.minimum(jnp.sum(tile_start[:, None] >= ends[None, :], axis=1), ne - 1).astype(jnp.int32)
    n_used = (ends[-1:] // tm).astype(jnp.int32)
    e12 = meta[:, 2:4].astype(jnp.int32)
    pos = (off[e12] + meta[:, 4:6].astype(jnp.int32)).T
    xs = _sc_scatter_rows(h2p, pos, n_tiles * tm)
    ys = _experts(xs, tile_expert, n_used, wg, wu, wd, layer, tm=tm)
    yg = _sc_gather_rows(ys, pos.reshape(2 * t))
    return _moe_out(xn, yg, meta, gf)


def _final_norm_kernel(x_ref, g_ref, o_ref):
    o_ref[...] = _rms(x_ref[...], g_ref[...])


def _final_norm(x2, g, *, tm=512):
    t, d = x2.shape
    tm = min(tm, t)
    return pl.pallas_call(
        _final_norm_kernel,
        grid=(t // tm,),
        in_specs=[pl.BlockSpec((tm, d), lambda i: (i, 0)), pl.BlockSpec((1, d), lambda i: (0, 0))],
        out_specs=pl.BlockSpec((tm, d), lambda i: (i, 0)),
        out_shape=jax.ShapeDtypeStruct((t, d), F32),
        compiler_params=_cparams(("parallel",)),
        name="final_norm",
    )(x2, g.reshape(1, d))


def kernel(x, mix_norm, w_in, hgrn_lb_logits, hgrn_out_norm, w_branch_hgrn, w_branch_sb, w_out, ffn_norm,
           dense_w_gate, dense_w_up, dense_w_down, moe_router, moe_w_gate, moe_w_up, moe_w_down, final_norm):
    b, s, d = x.shape
    t = b * s
    depth = w_in.shape[0]
    hgrn_width = HGRN_HEADS * HGRN_HEAD_DIM
    sb_width = SB_HEADS * SB_HEAD_DIM
    hgrn_plane = 0
    sb_plane = hgrn_plane + 4 * hgrn_width // LANES
    gate_plane = sb_plane + 3 * sb_width // LANES

    lb_all = jnp.cumsum(jax.nn.softmax(hgrn_lb_logits.astype(F32), axis=0), axis=0)
    lb_all = lb_all - lb_all[0:1]
    lb_params = jnp.stack([jnp.log(lb_all), jnp.log1p(-lb_all), 1.0 - lb_all], axis=1)
    lb_params = lb_params.reshape(depth, 3, HGRN_HEADS, HGRN_HEAD_DIM).transpose(0, 2, 1, 3)

    w_in, w_branch_hgrn, w_branch_sb, w_out, dense_w_gate, dense_w_up, dense_w_down = (
        w.astype(BF16) for w in (w_in, w_branch_hgrn, w_branch_sb, w_out, dense_w_gate, dense_w_up, dense_w_down))

    x2 = x.reshape(t, d)
    for layer in range(depth):
        planes = _inproj(x2, mix_norm[layer], w_in, layer)
        planes4 = planes.reshape(planes.shape[0], b, s, LANES)
        oa = _hgrn(planes4, lb_params[layer], hgrn_out_norm[layer], first_plane=hgrn_plane)
        ob = _sb_attention(planes4, first_plane=sb_plane)
        j = layer // 2
        moe = layer % 2 == 1
        wr = None
        if moe:
            wr = jnp.zeros((d, LANES), F32).at[:, :N_EXPERTS].set(moe_router[j].astype(F32))
            wr_hi = wr.astype(BF16)
            wr = jnp.stack([wr_hi, (wr - wr_hi.astype(F32)).astype(BF16)])
        res = _combine(x2, oa.reshape(HGRN_HEADS, t, LANES), ob.reshape(ob.shape[0], t, LANES), planes,
                       w_branch_hgrn, w_branch_sb, w_out, layer, ffn_norm[layer], wr, gate_plane=gate_plane)
        last = layer == depth - 1
        if moe:
            xn, h2p, meta, totals = res
            x2 = _moe(h2p, xn, meta, totals, moe_w_gate, moe_w_up, moe_w_down, j, final_norm if last else None)
        else:
            xn, h2 = res
            x2 = _swiglu(h2, xn, dense_w_gate, dense_w_up, dense_w_down, j)
            if last:
                x2 = _final_norm(x2, final_norm)
    return x2.reshape(b, s, d)
```

```python
import functools

import jax
import jax.numpy as jnp
from jax import lax
from jax.experimental import pallas as pl
from jax.experimental.pallas import tpu as pltpu
from jax.experimental.pallas import tpu_sc as plsc

F32 = jnp.float32
BF16 = jnp.bfloat16

LANES = 128
SUBLANES = 8
HGRN_HEADS = 4
HGRN_HEAD_DIM = 128
SB_HEADS = 8
SB_HEAD_DIM = 64
SB_HEADS_PER_PLANE = LANES // SB_HEAD_DIM
N_EXPERTS = 8
RMS_EPS = 1e-6
VMEM_LIMIT_BYTES = 56 * 1024 * 1024

HGRN_CHUNK = 128
HGRN_DIAG = 8
HGRN_ROWS = 512
HGRN_HEADS_PER_STEP = 4
SB_BLOCK = 256
FFN_TILE = 512
MOE_TILE = 1024
SC_CHUNK = 64
SC_BUFFERS = 3
HIGH_HALF_MASK = -65536
SB_PLANES = 2
LOG2_E = 1.4426950408889634
SB_UNDERFLOW_LOG = -105.0


def _cparams(semantics):
    return pltpu.CompilerParams(dimension_semantics=semantics, vmem_limit_bytes=VMEM_LIMIT_BYTES)


def _rms(x, g):
    ms = jnp.mean(x * x, axis=-1, keepdims=True)
    return x * lax.rsqrt(ms + RMS_EPS) * g


def _sigmoid(x):
    return 0.5 * jnp.tanh(0.5 * x) + 0.5


def _dot(a, b):
    return jnp.dot(a, b, preferred_element_type=F32)


def _dot_nt(a, b):
    return lax.dot_general(a, b, (((1,), (1,)), ((), ())), preferred_element_type=F32)


def _dot_tn(a, b):
    return lax.dot_general(a, b, (((0,), (0,)), ((), ())), preferred_element_type=F32)


def _inproj_kernel(x_ref, g_ref, w_ref, o_ref, *, tn):
    h = _rms(x_ref[...], g_ref[...]).astype(BF16)
    for j in range(w_ref.shape[1] // tn):
        r = _dot(h, w_ref[:, j * tn:(j + 1) * tn])
        for c in range(tn // LANES):
            o_ref[j * (tn // LANES) + c] = r[:, c * LANES:(c + 1) * LANES].astype(BF16)


def _inproj(x2, g, w, layer, *, tm=512, tn=512):
    t, d = x2.shape
    n = w.shape[2]
    tm = min(tm, t)
    return pl.pallas_call(
        functools.partial(_inproj_kernel, tn=tn),
        grid=(t // tm,),
        in_specs=[pl.BlockSpec((tm, d), lambda i: (i, 0)),
                  pl.BlockSpec((1, d), lambda i: (0, 0)),
                  pl.BlockSpec((None, d, n), lambda i: (layer, 0, 0))],
        out_specs=pl.BlockSpec((n // LANES, tm, LANES), lambda i: (0, i, 0)),
        out_shape=jax.ShapeDtypeStruct((n // LANES, t, LANES), BF16),
        compiler_params=_cparams(("parallel",)),
        name="inproj",
    )(x2, g.reshape(1, d), w)


def _hgrn_chunk(zq, zf, vi, zog, lbp, gn, st):
    c = zq.shape[0]
    log_lb, log1m_lb, one_m_lb = lbp[0:1], lbp[1:2], lbp[2:3]

    q = zq * _sigmoid(zq)
    log_sig = jnp.minimum(zf, 0.0) - jnp.log(1.0 + jnp.exp(-jnp.abs(zf)))
    k = one_m_lb * _sigmoid(-zf)
    cc = log1m_lb + log_sig
    log_f = jnp.maximum(log_lb, cc) + jnp.log(1.0 + jnp.exp(-jnp.abs(log_lb - cc)))
    log_f = log_f * LOG2_E
    yield

    hi = log_f.astype(BF16)
    r1 = log_f - hi.astype(F32)
    mid = r1.astype(BF16)
    lo = (r1 - mid.astype(F32)).astype(BF16)
    row = lax.broadcasted_iota(jnp.int32, (c, c), 0)
    col = lax.broadcasted_iota(jnp.int32, (c, c), 1)
    tri = jnp.where(col <= row, 1.0, 0.0).astype(BF16)
    parts = _dot(tri, jnp.concatenate([hi, mid, lo], axis=1))
    b = parts[:, :LANES] + parts[:, LANES:2 * LANES] + parts[:, 2 * LANES:]
    b_prev = b - log_f
    b_last = b[c - 1:c, :]
    yield

    o = _dot_nt((q * jnp.exp2(b)).astype(BF16), st.astype(BF16))
    k_tail = (k * jnp.exp2(b_last - b)).astype(BF16)
    st_new = st * jnp.exp2(b_last) + _dot_tn(vi, k_tail)
    yield

    rows = lax.broadcasted_iota(jnp.int32, (c, 1), 0)
    scores = jnp.zeros((c, c), F32)
    m = HGRN_DIAG
    while m < c:
        n = c // m
        base = b_prev.reshape(n, m, LANES)[:, 0:1, :]
        p = b.reshape(n, m, LANES) - base
        suf = (p[:, m - 1:m, :] - p).reshape(c, LANES)
        p = p.reshape(c, LANES)
        odd = ((rows // m) % 2) == 1
        q_m = jnp.where(odd, q * jnp.exp2(p), 0.0).astype(BF16)
        k_m = jnp.where(odd, 0.0, k * jnp.exp2(suf)).astype(BF16)
        s_m = _dot_nt(q_m, k_m)
        scores = scores + jnp.where((row // (2 * m)) == (col // (2 * m)), s_m, 0.0)
        m *= 2
        yield
    blocks = (c // HGRN_DIAG, HGRN_DIAG, LANES)
    q3, k3, b3 = (a.reshape(blocks) for a in (q, k, b))
    sub_diagonal = jnp.where((row // HGRN_DIAG) == (col // HGRN_DIAG), row - col, -1)
    for j in range(HGRN_DIAG):
        ks, bs = (k3, b3) if j == 0 else (pltpu.roll(k3, j, 1), pltpu.roll(b3, j, 1))
        s = jnp.sum(q3 * ks * jnp.exp2(b3 - bs), axis=-1, keepdims=True).reshape(c, 1)
        scores = jnp.where(sub_diagonal == j, s, scores)
        yield
    o = o + _dot(scores.astype(BF16), vi)

    out = _rms(o, gn) * (zog * _sigmoid(zog))
    return out.astype(BF16), st_new


def _hgrn_kernel(q_ref, f_ref, i_ref, og_ref, lb_ref, gn_ref, o_ref, st_ref):
    @pl.when(pl.program_id(2) == 0)
    def _():
        st_ref[...] = jnp.zeros_like(st_ref)

    gn = gn_ref[...]

    def body(i, carry):
        rows = pl.ds(pl.multiple_of(i * HGRN_CHUNK, HGRN_CHUNK), HGRN_CHUNK)
        heads = [_hgrn_chunk(q_ref[h, rows, :].astype(F32), f_ref[h, rows, :].astype(F32), i_ref[h, rows, :],
                             og_ref[h, rows, :].astype(F32), lb_ref[h], gn, st_ref[h])
                 for h in range(q_ref.shape[0])]
        pending = dict(enumerate(heads))
        while pending:
            for h, gen in list(pending.items()):
                try:
                    next(gen)
                except StopIteration as done:
                    out, st_new = done.value
                    o_ref[h, rows, :] = out
                    st_ref[h] = st_new
                    del pending[h]
        return carry

    lax.fori_loop(0, q_ref.shape[1] // HGRN_CHUNK, body, 0)


def _hgrn(planes, lb_params, gn, *, first_plane):
    _, b, s, _ = planes.shape
    rows = min(HGRN_ROWS, s)
    h = HGRN_HEADS
    n = HGRN_HEADS_PER_STEP

    def plane_spec(k):
        blk = (first_plane + k * h) // n
        return pl.BlockSpec((n, None, rows, LANES), lambda bi, hi, ci: (blk + hi, bi, ci, 0))

    return pl.pallas_call(
        _hgrn_kernel,
        grid=(b, h // n, s // rows),
        in_specs=[plane_spec(0), plane_spec(1), plane_spec(2), plane_spec(3),
                  pl.BlockSpec((n, 3, LANES), lambda bi, hi, ci: (hi, 0, 0)),
                  pl.BlockSpec((1, LANES), lambda bi, hi, ci: (0, 0))],
        out_specs=pl.BlockSpec((n, None, rows, LANES), lambda bi, hi, ci: (hi, bi, ci, 0)),
        out_shape=jax.ShapeDtypeStruct((h, b, s, LANES), BF16),
        scratch_shapes=[pltpu.VMEM((n, HGRN_HEAD_DIM, HGRN_HEAD_DIM), F32)],
        compiler_params=_cparams(("parallel", "parallel", "arbitrary")),
        name="hgrn2",
    )(planes, planes, planes, planes, lb_params, gn.reshape(1, LANES))


def _sb_kernel(q_ref, k_ref, v_ref, o_ref):
    n_planes, tq, _ = q_ref.shape
    tk = tq
    qi = pl.program_id(2)
    lane = lax.broadcasted_iota(jnp.int32, (1, LANES), 1)
    first = lane < SB_HEAD_DIM
    q_heads = []
    for p in range(n_planes):
        q2 = q_ref[p] * jnp.asarray(SB_HEAD_DIM ** -0.5, BF16)
        q_heads += [(p, jnp.where(first, q2, jnp.zeros_like(q2))), (p, jnp.where(first, jnp.zeros_like(q2), q2))]
    n_heads = len(q_heads)
    strict = lax.broadcasted_iota(jnp.int32, (tq, tk), 1) < lax.broadcasted_iota(jnp.int32, (tq, tk), 0)
    later = jnp.where(lax.broadcasted_iota(jnp.int32, (tk, tk), 0) > lax.broadcasted_iota(jnp.int32, (tk, tk), 1),
                      1.0, 0.0).astype(BF16)

    def block(kb, accs, sums, diagonal):
        rows = pl.ds(pl.multiple_of(kb * tk, tk), tk)
        zs = [_dot_nt(qh, k_ref[p, rows, :]) for p, qh in q_heads]
        log_keeps = []
        for z in zs:
            neg_z = -z
            log_keep = jnp.minimum(neg_z, 0.0) - jnp.log(1.0 + jnp.exp(jnp.minimum(z, neg_z)))
            if diagonal:
                log_keep = jnp.where(strict, log_keep, 0.0)
            log_keeps.append(log_keep)
        afters = [_dot(lk.astype(BF16), later) for lk in log_keeps]
        ws = []
        for h in range(n_heads):
            w = jnp.exp(zs[h] + log_keeps[h] + afters[h] + sums[h])
            if diagonal:
                w = jnp.where(strict, w, 0.0)
            ws.append(w.astype(BF16))
        new_accs = [accs[h] + _dot(ws[h], v_ref[q_heads[h][0], rows, :]) for h in range(n_heads)]
        new_sums = [sums[h] + jnp.sum(log_keeps[h], axis=-1, keepdims=True) for h in range(n_heads)]
        top = new_sums[0]
        for h in range(1, n_heads):
            top = jnp.maximum(top, new_sums[h])
        return tuple(new_accs), tuple(new_sums), (jnp.max(top) < SB_UNDERFLOW_LOG).astype(jnp.int32)

    zero_acc = jnp.zeros((tq, LANES), F32)
    zero_sum = jnp.zeros((tq, 1), F32)
    accs, sums, done = block(qi, (zero_acc,) * n_heads, (zero_sum,) * n_heads, True)

    def cond(carry):
        it, done, _, _ = carry
        return jnp.logical_and(it <= qi, done == 0)

    def body(carry):
        it, _, accs, sums = carry
        accs, sums, done = block(qi - it, accs, sums, False)
        return it + 1, done, accs, sums

    _, _, accs, _ = lax.while_loop(cond, body, (jnp.int32(1), done, accs, sums))
    for p in range(n_planes):
        o_ref[p] = jnp.where(first, accs[2 * p], accs[2 * p + 1]).astype(BF16)


def _sb_attention(planes, *, first_plane):
    _, b, s, _ = planes.shape
    hp = SB_HEADS // SB_HEADS_PER_PLANE
    tq = SB_BLOCK
    n = SB_PLANES

    def spec(rows, k):
        blk = (first_plane + k * hp) // n
        return pl.BlockSpec((n, None, rows, LANES), lambda bi, hi, qi: (blk + hi, bi, qi if rows == tq else 0, 0))

    return pl.pallas_call(
        _sb_kernel,
        grid=(b, hp // n, s // tq),
        in_specs=[spec(tq, 0), spec(s, 1), spec(s, 2)],
        out_specs=pl.BlockSpec((n, None, tq, LANES), lambda bi, hi, qi: (hi, bi, qi, 0)),
        out_shape=jax.ShapeDtypeStruct((hp, b, s, LANES), BF16),
        compiler_params=_cparams(("parallel", "parallel", "arbitrary")),
        name="stickbreak",
    )(planes, planes, planes)


def _pack_pair(x):
    n = x.shape[1] // 2
    lo = lax.bitcast_convert_type(x[:, :n].astype(BF16).astype(F32), jnp.int32)
    hi = lax.bitcast_convert_type(x[:, n:].astype(BF16).astype(F32), jnp.int32)
    return lax.shift_right_logical(lo, 16) | (hi & HIGH_HALF_MASK)


def _unpack_pair(p):
    lo = lax.bitcast_convert_type(lax.shift_left(p, 16), F32)
    hi = lax.bitcast_convert_type(p & HIGH_HALF_MASK, F32)
    return lo, hi


def _route(h, wr, run_ref):
    tm = h.shape[0]
    h_hi = h.astype(BF16)
    h_lo = (h - h_hi.astype(F32)).astype(BF16)
    r = _dot(h_hi, wr)
    logits = r[:, :LANES] + (r[:, LANES:] + _dot(h_lo, wr[:, :LANES]))
    lane = lax.broadcasted_iota(jnp.int32, (1, LANES), 1).astype(F32)
    neg = jnp.float32(-jnp.inf)
    lg = jnp.where(lane < N_EXPERTS, logits, neg)
    m1 = jnp.max(lg, axis=-1, keepdims=True)
    i1 = jnp.min(jnp.where(lg == m1, lane, float(LANES)), axis=-1, keepdims=True)
    sel1 = lane == i1
    lg2 = jnp.where(sel1, neg, lg)
    m2 = jnp.max(lg2, axis=-1, keepdims=True)
    i2 = jnp.min(jnp.where(lg2 == m2, lane, float(LANES)), axis=-1, keepdims=True)
    sel2 = lane == i2
    e2 = jnp.exp(m2 - m1)
    w1 = 1.0 / (1.0 + e2)
    w2 = e2 * w1

    both = jnp.where(jnp.logical_or(sel1, sel2), 1.0, 0.0)
    row = lax.broadcasted_iota(jnp.int32, (tm, tm), 0)
    col = lax.broadcasted_iota(jnp.int32, (tm, tm), 1)
    before = jnp.where(col < row, 1.0, 0.0).astype(BF16)
    cnt = run_ref[...] + _dot(before, both.astype(BF16))
    rank1 = jnp.sum(jnp.where(sel1, cnt, 0.0), axis=-1, keepdims=True)
    rank2 = jnp.sum(jnp.where(sel2, cnt, 0.0), axis=-1, keepdims=True)
    run_ref[...] = run_ref[...] + jnp.sum(both, axis=0, keepdims=True)

    meta = jnp.zeros((tm, LANES), F32)
    for k, v in enumerate((w1, w2, i1, i2, rank1, rank2)):
        meta = jnp.where(lane == float(k), v, meta)
    return meta


def _combine_kernel(*refs, with_router):
    if with_router:
        (x_ref, oa_ref, ob_ref, ga0_ref, ga1_ref, gb0_ref, gb1_ref, pa_ref, pb_ref, wo_ref, g2_ref, wr_ref,
         xo_ref, h2_ref, meta_ref, tot_ref, run_ref) = refs
    else:
        x_ref, oa_ref, ob_ref, ga0_ref, ga1_ref, gb0_ref, gb1_ref, pa_ref, pb_ref, wo_ref, xo_ref = refs

    def cat(*plane_refs):
        return jnp.concatenate([ref[c] for ref in plane_refs for c in range(ref.shape[0])], axis=1)

    ya = _dot(cat(oa_ref), pa_ref[...])
    yb = _dot(cat(ob_ref), pb_ref[...])
    y = _sigmoid(cat(ga0_ref, ga1_ref).astype(F32)) * ya + _sigmoid(cat(gb0_ref, gb1_ref).astype(F32)) * yb
    xn = x_ref[...] + _dot(y.astype(BF16), wo_ref[...])
    xo_ref[...] = xn
    if with_router:
        @pl.when(pl.program_id(0) == 0)
        def _():
            run_ref[...] = jnp.zeros_like(run_ref)

        h2 = _rms(xn, g2_ref[...])
        h2_ref[...] = _pack_pair(h2)
        meta_ref[...] = _route(h2, wr_ref[...], run_ref)
        tot_ref[...] = jnp.broadcast_to(run_ref[...], tot_ref.shape)


def _combine(x2, oa, ob, planes, pa, pb, wo, layer, *, gate_plane, g2=None, wr=None, tm=512):
    t, d = x2.shape
    tm = min(tm, t)
    na, nb = oa.shape[0], ob.shape[0]
    ng = d // LANES // 2
    with_router = wr is not None
    gblk = gate_plane // ng
    in_specs = [pl.BlockSpec((tm, d), lambda i: (i, 0)),
                pl.BlockSpec((na, tm, LANES), lambda i: (0, i, 0)),
                pl.BlockSpec((nb, tm, LANES), lambda i: (0, i, 0)),
                pl.BlockSpec((ng, tm, LANES), lambda i: (gblk, i, 0)),
                pl.BlockSpec((ng, tm, LANES), lambda i: (gblk + 1, i, 0)),
                pl.BlockSpec((ng, tm, LANES), lambda i: (gblk + 2, i, 0)),
                pl.BlockSpec((ng, tm, LANES), lambda i: (gblk + 3, i, 0)),
                pl.BlockSpec((None,) + pa.shape[1:], lambda i: (layer, 0, 0)),
                pl.BlockSpec((None,) + pb.shape[1:], lambda i: (layer, 0, 0)),
                pl.BlockSpec((None,) + wo.shape[1:], lambda i: (layer, 0, 0))]
    args = [x2, oa, ob, planes, planes, planes, planes, pa, pb, wo]
    out_specs = [pl.BlockSpec((tm, d), lambda i: (i, 0))]
    out_shape = [jax.ShapeDtypeStruct((t, d), F32)]
    scratch = []
    if with_router:
        in_specs += [pl.BlockSpec((1, d), lambda i: (0, 0)), pl.BlockSpec(wr.shape, lambda i: (0, 0))]
        args += [g2.reshape(1, d), wr]
        out_specs += [pl.BlockSpec((tm, d // 2), lambda i: (i, 0)),
                      pl.BlockSpec((tm, LANES), lambda i: (i, 0)),
                      pl.BlockSpec((SUBLANES, LANES), lambda i: (0, 0))]
        out_shape += [jax.ShapeDtypeStruct((t, d // 2), jnp.int32),
                      jax.ShapeDtypeStruct((t, LANES), F32),
                      jax.ShapeDtypeStruct((SUBLANES, LANES), F32)]
        scratch = [pltpu.VMEM((1, LANES), F32)]
    return pl.pallas_call(
        functools.partial(_combine_kernel, with_router=with_router),
        grid=(t // tm,),
        in_specs=in_specs, out_specs=out_specs, out_shape=out_shape, scratch_shapes=scratch,
        compiler_params=_cparams(("arbitrary",)),
        name="combine_router" if with_router else "combine",
    )(*args)


def _swiglu_kernel(x_ref, g_ref, wg_ref, wu_ref, wd_ref, o_ref, h_scr, acc_ref):
    j = pl.program_id(1)

    @pl.when(j == 0)
    def _():
        h_scr[...] = _rms(x_ref[...], g_ref[...]).astype(BF16)
        acc_ref[...] = jnp.zeros_like(acc_ref)

    h = h_scr[...]
    a = _dot(h, wg_ref[...].astype(BF16))
    u = _dot(h, wu_ref[...].astype(BF16))
    acc_ref[...] += _dot((a * _sigmoid(a) * u).astype(BF16), wd_ref[...].astype(BF16))

    @pl.when(j == pl.num_programs(1) - 1)
    def _():
        o_ref[...] = x_ref[...] + acc_ref[...]


def _swiglu(x2, g, wg, wu, wd, layer, *, tm=1024, tf=FFN_TILE):
    t, d = x2.shape
    f = wg.shape[2]
    tm = min(tm, t)
    return pl.pallas_call(
        _swiglu_kernel,
        grid=(t // tm, f // tf),
        in_specs=[pl.BlockSpec((tm, d), lambda i, j: (i, 0)),
                  pl.BlockSpec((1, d), lambda i, j: (0, 0)),
                  pl.BlockSpec((None, d, tf), lambda i, j: (layer, 0, j)),
                  pl.BlockSpec((None, d, tf), lambda i, j: (layer, 0, j)),
                  pl.BlockSpec((None, tf, d), lambda i, j: (layer, j, 0))],
        out_specs=pl.BlockSpec((tm, d), lambda i, j: (i, 0)),
        out_shape=jax.ShapeDtypeStruct((t, d), F32),
        scratch_shapes=[pltpu.VMEM((tm, d), BF16), pltpu.VMEM((tm, d), F32)],
        compiler_params=_cparams(("parallel", "arbitrary")),
        name="swiglu",
    )(x2, g.reshape(1, d), wg, wu, wd)


def _sc_workers():
    info = plsc.get_sparse_core_info()
    return info.num_cores, info.num_cores * info.num_subcores


def _sc_scatter_rows(rows, pos, n_out):
    t, w = rows.shape
    kk = pos.shape[0]
    nc, nw = _sc_workers()
    per_w = t // nw
    nchunk = per_w // SC_CHUNK
    mesh = plsc.VectorSubcoreMesh(core_axis_name="c", subcore_axis_name="s")

    @functools.partial(
        pl.kernel, mesh=mesh, out_type=jax.ShapeDtypeStruct((n_out, w), rows.dtype),
        scratch_types=[pltpu.VMEM((kk, nchunk, SC_CHUNK), jnp.int32), pltpu.VMEM((SC_BUFFERS, SC_CHUNK, w), rows.dtype),
                       pltpu.SemaphoreType.DMA((SC_BUFFERS,)), pltpu.SemaphoreType.DMA((SC_BUFFERS,))],
        name="moe_dispatch")
    def scatter_kernel(rows_hbm, pos_hbm, out_hbm, pos_v, rows_v, load_sem, store_sem):
        wid = lax.axis_index("s") * nc + lax.axis_index("c")
        base = wid * per_w
        pltpu.sync_copy(pos_hbm.at[wid], pos_v)

        def load(j, b):
            src = rows_hbm.at[pl.ds(pl.multiple_of(base + j * SC_CHUNK, SC_CHUNK), SC_CHUNK)]
            return pltpu.make_async_copy(src, rows_v.at[b], load_sem.at[b])

        def store(j, b, k):
            return pltpu.make_async_copy(rows_v.at[b], out_hbm.at[pos_v.at[k, j]], store_sem.at[b])

        for b in range(min(SC_BUFFERS, nchunk)):
            load(b, b).start()

        @pl.loop(0, nchunk, step=SC_BUFFERS)
        def _(g):
            for b in range(SC_BUFFERS):
                j = g + b

                @pl.when(j < nchunk)
                def _():
                    load(j, b).wait()
                    for k in range(kk):
                        store(j, b, k).start()
                    for k in range(kk):
                        store(j, b, k).wait()

                    @pl.when(j + SC_BUFFERS < nchunk)
                    def _():
                        load(j + SC_BUFFERS, b).start()

    return scatter_kernel(rows, pos.reshape(kk, nw, nchunk, SC_CHUNK).transpose(1, 0, 2, 3))


def _sc_gather_rows(table, idx):
    n = idx.shape[0]
    w = table.shape[1]
    nc, nw = _sc_workers()
    per_w = n // nw
    nchunk = per_w // SC_CHUNK
    mesh = plsc.VectorSubcoreMesh(core_axis_name="c", subcore_axis_name="s")

    @functools.partial(
        pl.kernel, mesh=mesh, out_type=jax.ShapeDtypeStruct((n, w), table.dtype),
        scratch_types=[pltpu.VMEM((nchunk, SC_CHUNK), jnp.int32), pltpu.VMEM((SC_BUFFERS, SC_CHUNK, w), table.dtype),
                       pltpu.SemaphoreType.DMA((SC_BUFFERS,)), pltpu.SemaphoreType.DMA((SC_BUFFERS,))],
        name="moe_collect")
    def gather_kernel(table_hbm, idx_hbm, out_hbm, idx_v, rows_v, gather_sem, store_sem):
        wid = lax.axis_index("s") * nc + lax.axis_index("c")
        base = wid * per_w
        pltpu.sync_copy(idx_hbm.at[wid], idx_v)

        def gather(j, b):
            return pltpu.make_async_copy(table_hbm.at[idx_v.at[j]], rows_v.at[b], gather_sem.at[b])

        def store(j, b):
            dst = out_hbm.at[pl.ds(pl.multiple_of(base + j * SC_CHUNK, SC_CHUNK), SC_CHUNK)]
            return pltpu.make_async_copy(rows_v.at[b], dst, store_sem.at[b])

        for b in range(min(SC_BUFFERS, nchunk)):
            gather(b, b).start()

        @pl.loop(0, nchunk, step=SC_BUFFERS)
        def _(g):
            for b in range(SC_BUFFERS):
                j = g + b

                @pl.when(j < nchunk)
                def _():
                    gather(j, b).wait()
                    store(j, b).start()
                    store(j, b).wait()

                    @pl.when(j + SC_BUFFERS < nchunk)
                    def _():
                        gather(j + SC_BUFFERS, b).start()

    return gather_kernel(table, idx.reshape(nw, nchunk, SC_CHUNK))


def _experts_kernel(te_ref, nu_ref, xs_ref, wg_ref, wu_ref, wd_ref, ys_ref, h_scr, acc_ref):
    i = pl.program_id(0)
    j = pl.program_id(1)

    @pl.when(i < nu_ref[0])
    def _():
        @pl.when(j == 0)
        def _():
            lo, hi = _unpack_pair(xs_ref[...])
            half = lo.shape[1]
            h_scr[:, :half] = lo.astype(BF16)
            h_scr[:, half:] = hi.astype(BF16)
            acc_ref[...] = jnp.zeros_like(acc_ref)

        h = h_scr[...]
        a = _dot(h, wg_ref[...].astype(BF16))
        u = _dot(h, wu_ref[...].astype(BF16))
        acc_ref[...] += _dot((a * _sigmoid(a) * u).astype(BF16), wd_ref[...].astype(BF16))

        @pl.when(j == pl.num_programs(1) - 1)
        def _():
            ys_ref[...] = _pack_pair(acc_ref[...])


def _experts(xs, tile_expert, n_used, wg, wu, wd, layer, *, tm, tf=FFN_TILE):
    r, half = xs.shape
    d = 2 * half
    f = wg.shape[3]
    grid_spec = pltpu.PrefetchScalarGridSpec(
        num_scalar_prefetch=2,
        grid=(r // tm, f // tf),
        in_specs=[pl.BlockSpec((tm, half), lambda i, j, te, nu: (i, 0)),
                  pl.BlockSpec((None, None, d, tf), lambda i, j, te, nu: (layer, te[i], 0, j)),
                  pl.BlockSpec((None, None, d, tf), lambda i, j, te, nu: (layer, te[i], 0, j)),
                  pl.BlockSpec((None, None, tf, d), lambda i, j, te, nu: (layer, te[i], j, 0))],
        out_specs=pl.BlockSpec((tm, half), lambda i, j, te, nu: (i, 0)),
        scratch_shapes=[pltpu.VMEM((tm, d), BF16), pltpu.VMEM((tm, d), F32)])
    return pl.pallas_call(
        _experts_kernel,
        grid_spec=grid_spec,
        out_shape=jax.ShapeDtypeStruct((r, half), jnp.int32),
        compiler_params=_cparams(("arbitrary", "arbitrary")),
        name="moe_experts",
    )(tile_expert, n_used, xs, wg, wu, wd)


def _moe_out_kernel(*refs, final_norm):
    if final_norm:
        x_ref, y1_ref, y2_ref, meta_ref, gf_ref, o_ref = refs
    else:
        x_ref, y1_ref, y2_ref, meta_ref, o_ref = refs
    meta = meta_ref[...]
    w1, w2 = meta[:, 0:1], meta[:, 1:2]
    lo1, hi1 = _unpack_pair(y1_ref[...])
    lo2, hi2 = _unpack_pair(y2_ref[...])
    half = lo1.shape[1]
    x = x_ref[...]
    y = jnp.concatenate([x[:, :half] + w1 * lo1 + w2 * lo2, x[:, half:] + w1 * hi1 + w2 * hi2], axis=1)
    if final_norm:
        y = _rms(y, gf_ref[...])
    o_ref[...] = y


def _moe_out(x2, yg, meta, gf, *, tm=512):
    t, d = x2.shape
    tm = min(tm, t)
    nblk = t // tm
    final_norm = gf is not None
    in_specs = [pl.BlockSpec((tm, d), lambda i: (i, 0)),
                pl.BlockSpec((tm, d // 2), lambda i: (i, 0)),
                pl.BlockSpec((tm, d // 2), lambda i: (i + nblk, 0)),
                pl.BlockSpec((tm, LANES), lambda i: (i, 0))]
    args = [x2, yg, yg, meta]
    if final_norm:
        in_specs.append(pl.BlockSpec((1, d), lambda i: (0, 0)))
        args.append(gf.reshape(1, d))
    return pl.pallas_call(
        functools.partial(_moe_out_kernel, final_norm=final_norm),
        grid=(nblk,),
        in_specs=in_specs,
        out_specs=pl.BlockSpec((tm, d), lambda i: (i, 0)),
        out_shape=jax.ShapeDtypeStruct((t, d), F32),
        compiler_params=_cparams(("parallel",)),
        name="moe_out",
    )(*args)


def _moe(h2p, xn, meta, totals, wg, wu, wd, layer, gf, *, tm=MOE_TILE):
    t = xn.shape[0]
    tm = min(tm, t)
    ne = wg.shape[1]
    cnt = totals[0, :ne].astype(jnp.int32)
    cap = ((cnt + tm - 1) // tm) * tm
    ends = jnp.cumsum(cap)
    off = ends - cap
    n_tiles = 2 * t // tm + ne
    tile_start = jnp.arange(n_tiles, dtype=jnp.int32) * tm
    tile_expert = jnp.minimum(jnp.sum(tile_start[:, None] >= ends[None, :], axis=1), ne - 1).astype(jnp.int32)
    n_used = (ends[-1:] // tm).astype(jnp.int32)
    e12 = meta[:, 2:4].astype(jnp.int32)
    pos = (off[e12] + meta[:, 4:6].astype(jnp.int32)).T
    xs = _sc_scatter_rows(h2p, pos, n_tiles * tm)
    ys = _experts(xs, tile_expert, n_used, wg, wu, wd, layer, tm=tm)
    yg = _sc_gather_rows(ys, pos.reshape(2 * t))
    return _moe_out(xn, yg, meta, gf)


def _final_norm_kernel(x_ref, g_ref, o_ref):
    o_ref[...] = _rms(x_ref[...], g_ref[...])


def _final_norm(x2, g, *, tm=512):
    t, d = x2.shape
    tm = min(tm, t)
    return pl.pallas_call(
        _final_norm_kernel,
        grid=(t // tm,),
        in_specs=[pl.BlockSpec((tm, d), lambda i: (i, 0)), pl.BlockSpec((1, d), lambda i: (0, 0))],
        out_specs=pl.BlockSpec((tm, d), lambda i: (i, 0)),
        out_shape=jax.ShapeDtypeStruct((t, d), F32),
        compiler_params=_cparams(("parallel",)),
        name="final_norm",
    )(x2, g.reshape(1, d))


def kernel(x, mix_norm, w_in, hgrn_lb_logits, hgrn_out_norm, w_branch_hgrn, w_branch_sb, w_out, ffn_norm,
           dense_w_gate, dense_w_up, dense_w_down, moe_router, moe_w_gate, moe_w_up, moe_w_down, final_norm):
    b, s, d = x.shape
    t = b * s
    depth = w_in.shape[0]
    hgrn_width = HGRN_HEADS * HGRN_HEAD_DIM
    sb_width = SB_HEADS * SB_HEAD_DIM
    hgrn_plane = 0
    sb_plane = hgrn_plane + 4 * hgrn_width // LANES
    gate_plane = sb_plane + 3 * sb_width // LANES

    lb_all = jnp.cumsum(jax.nn.softmax(hgrn_lb_logits.astype(F32), axis=0), axis=0)
    lb_all = lb_all - lb_all[0:1]
    lb_params = jnp.stack([jnp.log(lb_all), jnp.log1p(-lb_all), 1.0 - lb_all], axis=1)
    lb_params = lb_params.reshape(depth, 3, HGRN_HEADS, HGRN_HEAD_DIM).transpose(0, 2, 1, 3)

    w_in, w_branch_hgrn, w_branch_sb, w_out = (w.astype(BF16) for w in (w_in, w_branch_hgrn, w_branch_sb, w_out))

    x2 = x.reshape(t, d)
    for layer in range(depth):
        planes = _inproj(x2, mix_norm[layer], w_in, layer)
        planes4 = planes.reshape(planes.shape[0], b, s, LANES)
        oa = _hgrn(planes4, lb_params[layer], hgrn_out_norm[layer], first_plane=hgrn_plane)
        ob = _sb_attention(planes4, first_plane=sb_plane)
        j = layer // 2
        moe = layer % 2 == 1
        wr = None
        if moe:
            wr = jnp.zeros((d, LANES), F32).at[:, :N_EXPERTS].set(moe_router[j].astype(F32))
            wr_hi = wr.astype(BF16)
            wr = jnp.concatenate([wr_hi, (wr - wr_hi.astype(F32)).astype(BF16)], axis=1)
        res = _combine(x2, oa.reshape(HGRN_HEADS, t, LANES), ob.reshape(ob.shape[0], t, LANES), planes,
                       w_branch_hgrn, w_branch_sb, w_out, layer, gate_plane=gate_plane,
                       g2=ffn_norm[layer] if moe else None, wr=wr)
        last = layer == depth - 1
        if moe:
            xn, h2p, meta, totals = res
            x2 = _moe(h2p, xn, meta, totals, moe_w_gate, moe_w_up, moe_w_down, j, final_norm if last else None)
        else:
            (xn,) = res
            x2 = _swiglu(xn, ffn_norm[layer], dense_w_gate, dense_w_up, dense_w_down, j)
            if last:
                x2 = _final_norm(x2, final_norm)
    return x2.reshape(b, s, d)
```

```python
import functools

import jax
import jax.numpy as jnp
from jax import lax
from jax.experimental import pallas as pl
from jax.experimental.pallas import tpu as pltpu
from jax.experimental.pallas import tpu_sc as plsc

F32 = jnp.float32
BF16 = jnp.bfloat16

LANES = 128
SUBLANES = 8
HGRN_HEADS = 4
HGRN_HEAD_DIM = 128
SB_HEADS = 8
SB_HEAD_DIM = 64
SB_HEADS_PER_PLANE = LANES // SB_HEAD_DIM
N_EXPERTS = 8
RMS_EPS = 1e-6
VMEM_LIMIT_BYTES = 56 * 1024 * 1024

HGRN_CHUNK = 128
HGRN_DIAG = 8
HGRN_ROWS = 512
HGRN_HEADS_PER_STEP = 4
SB_BLOCK = 256
FFN_TILE = 512
MOE_TILE = 1024
SC_CHUNK = 64
SC_BUFFERS = 3
HIGH_HALF_MASK = -65536
SB_PLANES = 2
LOG2_E = 1.4426950408889634
SB_UNDERFLOW_LOG = -105.0


def _cparams(semantics):
    return pltpu.CompilerParams(dimension_semantics=semantics, vmem_limit_bytes=VMEM_LIMIT_BYTES)


def _rms(x, g):
    ms = jnp.mean(x * x, axis=-1, keepdims=True)
    return x * lax.rsqrt(ms + RMS_EPS) * g


def _sigmoid(x):
    return 0.5 * jnp.tanh(0.5 * x) + 0.5


def _dot(a, b):
    return jnp.dot(a, b, preferred_element_type=F32)


def _dot_nt(a, b):
    return lax.dot_general(a, b, (((1,), (1,)), ((), ())), preferred_element_type=F32)


def _dot_tn(a, b):
    return lax.dot_general(a, b, (((0,), (0,)), ((), ())), preferred_element_type=F32)


def _inproj_kernel(x_ref, g_ref, w_ref, o_ref, *, tn):
    h = _rms(x_ref[...], g_ref[...]).astype(BF16)
    for j in range(w_ref.shape[1] // tn):
        r = _dot(h, w_ref[:, j * tn:(j + 1) * tn])
        for c in range(tn // LANES):
            o_ref[j * (tn // LANES) + c] = r[:, c * LANES:(c + 1) * LANES].astype(BF16)


def _inproj(x2, g, w, layer, *, tm=512, tn=512):
    t, d = x2.shape
    n = w.shape[2]
    tm = min(tm, t)
    return pl.pallas_call(
        functools.partial(_inproj_kernel, tn=tn),
        grid=(t // tm,),
        in_specs=[pl.BlockSpec((tm, d), lambda i: (i, 0)),
                  pl.BlockSpec((1, d), lambda i: (0, 0)),
                  pl.BlockSpec((None, d, n), lambda i: (layer, 0, 0))],
        out_specs=pl.BlockSpec((n // LANES, tm, LANES), lambda i: (0, i, 0)),
        out_shape=jax.ShapeDtypeStruct((n // LANES, t, LANES), BF16),
        compiler_params=_cparams(("parallel",)),
        name="inproj",
    )(x2, g.reshape(1, d), w)


def _hgrn_chunk(zq, zf, vi, zog, lbp, gn, st):
    c = zq.shape[0]
    log_lb, log1m_lb, one_m_lb = lbp[0:1], lbp[1:2], lbp[2:3]

    q = zq * _sigmoid(zq)
    log_sig = jnp.minimum(zf, 0.0) - jnp.log(1.0 + jnp.exp(-jnp.abs(zf)))
    k = one_m_lb * _sigmoid(-zf)
    cc = log1m_lb + log_sig
    log_f = jnp.maximum(log_lb, cc) + jnp.log(1.0 + jnp.exp(-jnp.abs(log_lb - cc)))
    log_f = log_f * LOG2_E
    yield

    hi = log_f.astype(BF16)
    r1 = log_f - hi.astype(F32)
    mid = r1.astype(BF16)
    lo = (r1 - mid.astype(F32)).astype(BF16)
    row = lax.broadcasted_iota(jnp.int32, (c, c), 0)
    col = lax.broadcasted_iota(jnp.int32, (c, c), 1)
    tri = jnp.where(col <= row, 1.0, 0.0).astype(BF16)
    parts = _dot(tri, jnp.concatenate([hi, mid, lo], axis=1))
    b = parts[:, :LANES] + parts[:, LANES:2 * LANES] + parts[:, 2 * LANES:]
    b_prev = b - log_f
    b_last = b[c - 1:c, :]
    yield

    o = _dot_nt((q * jnp.exp2(b)).astype(BF16), st.astype(BF16))
    k_tail = (k * jnp.exp2(b_last - b)).astype(BF16)
    st_new = st * jnp.exp2(b_last) + _dot_tn(vi, k_tail)
    yield

    rows = lax.broadcasted_iota(jnp.int32, (c, 1), 0)
    scores = jnp.zeros((c, c), F32)
    m = HGRN_DIAG
    while m < c:
        n = c // m
        base = b_prev.reshape(n, m, LANES)[:, 0:1, :]
        p = b.reshape(n, m, LANES) - base
        suf = (p[:, m - 1:m, :] - p).reshape(c, LANES)
        p = p.reshape(c, LANES)
        odd = ((rows // m) % 2) == 1
        q_m = jnp.where(odd, q * jnp.exp2(p), 0.0).astype(BF16)
        k_m = jnp.where(odd, 0.0, k * jnp.exp2(suf)).astype(BF16)
        s_m = _dot_nt(q_m, k_m)
        scores = scores + jnp.where((row // (2 * m)) == (col // (2 * m)), s_m, 0.0)
        m *= 2
        yield
    blocks = (c // HGRN_DIAG, HGRN_DIAG, LANES)
    q3, k3, b3 = (a.reshape(blocks) for a in (q, k, b))
    sub_diagonal = jnp.where((row // HGRN_DIAG) == (col // HGRN_DIAG), row - col, -1)
    for j in range(HGRN_DIAG):
        ks, bs = (k3, b3) if j == 0 else (pltpu.roll(k3, j, 1), pltpu.roll(b3, j, 1))
        s = jnp.sum(q3 * ks * jnp.exp2(b3 - bs), axis=-1, keepdims=True).reshape(c, 1)
        scores = jnp.where(sub_diagonal == j, s, scores)
        yield
    o = o + _dot(scores.astype(BF16), vi)

    out = _rms(o, gn) * (zog * _sigmoid(zog))
    return out.astype(BF16), st_new


def _hgrn_kernel(q_ref, f_ref, i_ref, og_ref, lb_ref, gn_ref, o_ref, st_ref):
    @pl.when(pl.program_id(2) == 0)
    def _():
        st_ref[...] = jnp.zeros_like(st_ref)

    gn = gn_ref[...]

    def body(i, carry):
        rows = pl.ds(pl.multiple_of(i * HGRN_CHUNK, HGRN_CHUNK), HGRN_CHUNK)
        heads = [_hgrn_chunk(q_ref[h, rows, :].astype(F32), f_ref[h, rows, :].astype(F32), i_ref[h, rows, :],
                             og_ref[h, rows, :].astype(F32), lb_ref[h], gn, st_ref[h])
                 for h in range(q_ref.shape[0])]
        pending = dict(enumerate(heads))
        while pending:
            for h, gen in list(pending.items()):
                try:
                    next(gen)
                except StopIteration as done:
                    out, st_new = done.value
                    o_ref[h, rows, :] = out
                    st_ref[h] = st_new
                    del pending[h]
        return carry

    lax.fori_loop(0, q_ref.shape[1] // HGRN_CHUNK, body, 0)


def _hgrn(planes, lb_params, gn, *, first_plane):
    _, b, s, _ = planes.shape
    rows = min(HGRN_ROWS, s)
    h = HGRN_HEADS
    n = HGRN_HEADS_PER_STEP

    def plane_spec(k):
        blk = (first_plane + k * h) // n
        return pl.BlockSpec((n, None, rows, LANES), lambda bi, hi, ci: (blk + hi, bi, ci, 0))

    return pl.pallas_call(
        _hgrn_kernel,
        grid=(b, h // n, s // rows),
        in_specs=[plane_spec(0), plane_spec(1), plane_spec(2), plane_spec(3),
                  pl.BlockSpec((n, 3, LANES), lambda bi, hi, ci: (hi, 0, 0)),
                  pl.BlockSpec((1, LANES), lambda bi, hi, ci: (0, 0))],
        out_specs=pl.BlockSpec((n, None, rows, LANES), lambda bi, hi, ci: (hi, bi, ci, 0)),
        out_shape=jax.ShapeDtypeStruct((h, b, s, LANES), BF16),
        scratch_shapes=[pltpu.VMEM((n, HGRN_HEAD_DIM, HGRN_HEAD_DIM), F32)],
        compiler_params=_cparams(("parallel", "parallel", "arbitrary")),
        name="hgrn2",
    )(planes, planes, planes, planes, lb_params, gn.reshape(1, LANES))


def _sb_kernel(q_ref, k_ref, v_ref, o_ref):
    n_planes, tq, _ = q_ref.shape
    tk = tq
    qi = pl.program_id(2)
    lane = lax.broadcasted_iota(jnp.int32, (1, LANES), 1)
    first = lane < SB_HEAD_DIM
    q_heads = []
    for p in range(n_planes):
        q2 = q_ref[p] * jnp.asarray(SB_HEAD_DIM ** -0.5, BF16)
        q_heads += [(p, jnp.where(first, q2, jnp.zeros_like(q2))), (p, jnp.where(first, jnp.zeros_like(q2), q2))]
    n_heads = len(q_heads)
    strict = lax.broadcasted_iota(jnp.int32, (tq, tk), 1) < lax.broadcasted_iota(jnp.int32, (tq, tk), 0)
    later = jnp.where(lax.broadcasted_iota(jnp.int32, (tk, tk), 0) > lax.broadcasted_iota(jnp.int32, (tk, tk), 1),
                      1.0, 0.0).astype(BF16)

    def block(kb, accs, sums, diagonal):
        rows = pl.ds(pl.multiple_of(kb * tk, tk), tk)
        zs = [_dot_nt(qh, k_ref[p, rows, :]) for p, qh in q_heads]
        log_keeps = []
        for z in zs:
            neg_z = -z
            log_keep = jnp.minimum(neg_z, 0.0) - jnp.log(1.0 + jnp.exp(jnp.minimum(z, neg_z)))
            if diagonal:
                log_keep = jnp.where(strict, log_keep, 0.0)
            log_keeps.append(log_keep)
        afters = [_dot(lk.astype(BF16), later) for lk in log_keeps]
        ws = []
        for h in range(n_heads):
            w = jnp.exp(zs[h] + log_keeps[h] + afters[h] + sums[h])
            if diagonal:
                w = jnp.where(strict, w, 0.0)
            ws.append(w.astype(BF16))
        new_accs = [accs[h] + _dot(ws[h], v_ref[q_heads[h][0], rows, :]) for h in range(n_heads)]
        new_sums = [sums[h] + jnp.sum(log_keeps[h], axis=-1, keepdims=True) for h in range(n_heads)]
        top = new_sums[0]
        for h in range(1, n_heads):
            top = jnp.maximum(top, new_sums[h])
        return tuple(new_accs), tuple(new_sums), (jnp.max(top) < SB_UNDERFLOW_LOG).astype(jnp.int32)

    zero_acc = jnp.zeros((tq, LANES), F32)
    zero_sum = jnp.zeros((tq, 1), F32)
    accs, sums, done = block(qi, (zero_acc,) * n_heads, (zero_sum,) * n_heads, True)

    def cond(carry):
        it, done, _, _ = carry
        return jnp.logical_and(it <= qi, done == 0)

    def body(carry):
        it, _, accs, sums = carry
        accs, sums, done = block(qi - it, accs, sums, False)
        return it + 1, done, accs, sums

    _, _, accs, _ = lax.while_loop(cond, body, (jnp.int32(1), done, accs, sums))
    for p in range(n_planes):
        o_ref[p] = jnp.where(first, accs[2 * p], accs[2 * p + 1]).astype(BF16)


def _sb_attention(planes, *, first_plane):
    _, b, s, _ = planes.shape
    hp = SB_HEADS // SB_HEADS_PER_PLANE
    tq = SB_BLOCK
    n = SB_PLANES

    def spec(rows, k):
        blk = (first_plane + k * hp) // n
        return pl.BlockSpec((n, None, rows, LANES), lambda bi, hi, qi: (blk + hi, bi, qi if rows == tq else 0, 0))

    return pl.pallas_call(
        _sb_kernel,
        grid=(b, hp // n, s // tq),
        in_specs=[spec(tq, 0), spec(s, 1), spec(s, 2)],
        out_specs=pl.BlockSpec((n, None, tq, LANES), lambda bi, hi, qi: (hi, bi, qi, 0)),
        out_shape=jax.ShapeDtypeStruct((hp, b, s, LANES), BF16),
        compiler_params=_cparams(("parallel", "parallel", "arbitrary")),
        name="stickbreak",
    )(planes, planes, planes)


def _pack_pair(x):
    n = x.shape[1] // 2
    lo = lax.bitcast_convert_type(x[:, :n].astype(BF16).astype(F32), jnp.int32)
    hi = lax.bitcast_convert_type(x[:, n:].astype(BF16).astype(F32), jnp.int32)
    return lax.shift_right_logical(lo, 16) | (hi & HIGH_HALF_MASK)


def _unpack_pair(p):
    lo = lax.bitcast_convert_type(lax.shift_left(p, 16), F32)
    hi = lax.bitcast_convert_type(p & HIGH_HALF_MASK, F32)
    return lo, hi


def _route(h, wr, run_ref):
    tm = h.shape[0]
    h_hi = h.astype(BF16)
    h_lo = (h - h_hi.astype(F32)).astype(BF16)
    r = _dot(h_hi, wr)
    logits = r[:, :LANES] + (r[:, LANES:] + _dot(h_lo, wr[:, :LANES]))
    lane = lax.broadcasted_iota(jnp.int32, (1, LANES), 1).astype(F32)
    neg = jnp.float32(-jnp.inf)
    lg = jnp.where(lane < N_EXPERTS, logits, neg)
    m1 = jnp.max(lg, axis=-1, keepdims=True)
    i1 = jnp.min(jnp.where(lg == m1, lane, float(LANES)), axis=-1, keepdims=True)
    sel1 = lane == i1
    lg2 = jnp.where(sel1, neg, lg)
    m2 = jnp.max(lg2, axis=-1, keepdims=True)
    i2 = jnp.min(jnp.where(lg2 == m2, lane, float(LANES)), axis=-1, keepdims=True)
    sel2 = lane == i2
    e2 = jnp.exp(m2 - m1)
    w1 = 1.0 / (1.0 + e2)
    w2 = e2 * w1

    both = jnp.where(jnp.logical_or(sel1, sel2), 1.0, 0.0)
    row = lax.broadcasted_iota(jnp.int32, (tm, tm), 0)
    col = lax.broadcasted_iota(jnp.int32, (tm, tm), 1)
    before = jnp.where(col < row, 1.0, 0.0).astype(BF16)
    cnt = run_ref[...] + _dot(before, both.astype(BF16))
    rank1 = jnp.sum(jnp.where(sel1, cnt, 0.0), axis=-1, keepdims=True)
    rank2 = jnp.sum(jnp.where(sel2, cnt, 0.0), axis=-1, keepdims=True)
    run_ref[...] = run_ref[...] + jnp.sum(both, axis=0, keepdims=True)

    meta = jnp.zeros((tm, LANES), F32)
    for k, v in enumerate((w1, w2, i1, i2, rank1, rank2)):
        meta = jnp.where(lane == float(k), v, meta)
    return meta


def _combine_kernel(*refs, with_router):
    if with_router:
        (x_ref, oa_ref, ob_ref, ga0_ref, ga1_ref, gb0_ref, gb1_ref, pa_ref, pb_ref, wo_ref, g2_ref, wr_ref,
         xo_ref, h2_ref, meta_ref, tot_ref, run_ref) = refs
    else:
        x_ref, oa_ref, ob_ref, ga0_ref, ga1_ref, gb0_ref, gb1_ref, pa_ref, pb_ref, wo_ref, xo_ref = refs

    def cat(*plane_refs):
        return jnp.concatenate([ref[c] for ref in plane_refs for c in range(ref.shape[0])], axis=1)

    ya = _dot(cat(oa_ref), pa_ref[...])
    yb = _dot(cat(ob_ref), pb_ref[...])
    y = _sigmoid(cat(ga0_ref, ga1_ref).astype(F32)) * ya + _sigmoid(cat(gb0_ref, gb1_ref).astype(F32)) * yb
    xn = x_ref[...] + _dot(y.astype(BF16), wo_ref[...])
    xo_ref[...] = xn
    if with_router:
        @pl.when(pl.program_id(0) == 0)
        def _():
            run_ref[...] = jnp.zeros_like(run_ref)

        h2 = _rms(xn, g2_ref[...])
        h2_ref[...] = _pack_pair(h2)
        meta_ref[...] = _route(h2, wr_ref[...], run_ref)
        tot_ref[...] = jnp.broadcast_to(run_ref[...], tot_ref.shape)


def _combine(x2, oa, ob, planes, pa, pb, wo, layer, *, gate_plane, g2=None, wr=None, tm=512):
    t, d = x2.shape
    tm = min(tm, t)
    na, nb = oa.shape[0], ob.shape[0]
    ng = d // LANES // 2
    with_router = wr is not None
    gblk = gate_plane // ng
    in_specs = [pl.BlockSpec((tm, d), lambda i: (i, 0)),
                pl.BlockSpec((na, tm, LANES), lambda i: (0, i, 0)),
                pl.BlockSpec((nb, tm, LANES), lambda i: (0, i, 0)),
                pl.BlockSpec((ng, tm, LANES), lambda i: (gblk, i, 0)),
                pl.BlockSpec((ng, tm, LANES), lambda i: (gblk + 1, i, 0)),
                pl.BlockSpec((ng, tm, LANES), lambda i: (gblk + 2, i, 0)),
                pl.BlockSpec((ng, tm, LANES), lambda i: (gblk + 3, i, 0)),
                pl.BlockSpec((None,) + pa.shape[1:], lambda i: (layer, 0, 0)),
                pl.BlockSpec((None,) + pb.shape[1:], lambda i: (layer, 0, 0)),
                pl.BlockSpec((None,) + wo.shape[1:], lambda i: (layer, 0, 0))]
    args = [x2, oa, ob, planes, planes, planes, planes, pa, pb, wo]
    out_specs = [pl.BlockSpec((tm, d), lambda i: (i, 0))]
    out_shape = [jax.ShapeDtypeStruct((t, d), F32)]
    scratch = []
    if with_router:
        in_specs += [pl.BlockSpec((1, d), lambda i: (0, 0)), pl.BlockSpec(wr.shape, lambda i: (0, 0))]
        args += [g2.reshape(1, d), wr]
        out_specs += [pl.BlockSpec((tm, d // 2), lambda i: (i, 0)),
                      pl.BlockSpec((tm, LANES), lambda i: (i, 0)),
                      pl.BlockSpec((SUBLANES, LANES), lambda i: (0, 0))]
        out_shape += [jax.ShapeDtypeStruct((t, d // 2), jnp.int32),
                      jax.ShapeDtypeStruct((t, LANES), F32),
                      jax.ShapeDtypeStruct((SUBLANES, LANES), F32)]
        scratch = [pltpu.VMEM((1, LANES), F32)]
    return pl.pallas_call(
        functools.partial(_combine_kernel, with_router=with_router),
        grid=(t // tm,),
        in_specs=in_specs, out_specs=out_specs, out_shape=out_shape, scratch_shapes=scratch,
        compiler_params=_cparams(("arbitrary",)),
        name="combine_router" if with_router else "combine",
    )(*args)


def _swiglu_kernel(x_ref, g_ref, wg_ref, wu_ref, wd_ref, o_ref, h_scr, acc_ref):
    j = pl.program_id(1)

    @pl.when(j == 0)
    def _():
        h_scr[...] = _rms(x_ref[...], g_ref[...]).astype(BF16)
        acc_ref[...] = jnp.zeros_like(acc_ref)

    h = h_scr[...]
    a = _dot(h, wg_ref[...].astype(BF16))
    u = _dot(h, wu_ref[...].astype(BF16))
    acc_ref[...] += _dot((a * _sigmoid(a) * u).astype(BF16), wd_ref[...].astype(BF16))

    @pl.when(j == pl.num_programs(1) - 1)
    def _():
        o_ref[...] = x_ref[...] + acc_ref[...]


def _swiglu(x2, g, wg, wu, wd, layer, *, tm=1024, tf=FFN_TILE):
    t, d = x2.shape
    f = wg.shape[2]
    tm = min(tm, t)
    return pl.pallas_call(
        _swiglu_kernel,
        grid=(t // tm, f // tf),
        in_specs=[pl.BlockSpec((tm, d), lambda i, j: (i, 0)),
                  pl.BlockSpec((1, d), lambda i, j: (0, 0)),
                  pl.BlockSpec((None, d, tf), lambda i, j: (layer, 0, j)),
                  pl.BlockSpec((None, d, tf), lambda i, j: (layer, 0, j)),
                  pl.BlockSpec((None, tf, d), lambda i, j: (layer, j, 0))],
        out_specs=pl.BlockSpec((tm, d), lambda i, j: (i, 0)),
        out_shape=jax.ShapeDtypeStruct((t, d), F32),
        scratch_shapes=[pltpu.VMEM((tm, d), BF16), pltpu.VMEM((tm, d), F32)],
        compiler_params=_cparams(("parallel", "arbitrary")),
        name="swiglu",
    )(x2, g.reshape(1, d), wg, wu, wd)


def _sc_workers():
    info = plsc.get_sparse_core_info()
    return info.num_cores, info.num_cores * info.num_subcores


def _sc_scatter_rows(rows, pos, n_out):
    t, w = rows.shape
    kk = pos.shape[0]
    nc, nw = _sc_workers()
    per_w = t // nw
    nchunk = per_w // SC_CHUNK
    mesh = plsc.VectorSubcoreMesh(core_axis_name="c", subcore_axis_name="s")

    @functools.partial(
        pl.kernel, mesh=mesh, out_type=jax.ShapeDtypeStruct((n_out, w), rows.dtype),
        scratch_types=[pltpu.VMEM((kk, nchunk, SC_CHUNK), jnp.int32), pltpu.VMEM((SC_BUFFERS, SC_CHUNK, w), rows.dtype),
                       pltpu.SemaphoreType.DMA((SC_BUFFERS,)), pltpu.SemaphoreType.DMA((SC_BUFFERS,))],
        name="moe_dispatch")
    def scatter_kernel(rows_hbm, pos_hbm, out_hbm, pos_v, rows_v, load_sem, store_sem):
        wid = lax.axis_index("s") * nc + lax.axis_index("c")
        base = wid * per_w
        pltpu.sync_copy(pos_hbm.at[wid], pos_v)

        def load(j, b):
            src = rows_hbm.at[pl.ds(pl.multiple_of(base + j * SC_CHUNK, SC_CHUNK), SC_CHUNK)]
            return pltpu.make_async_copy(src, rows_v.at[b], load_sem.at[b])

        def store(j, b, k):
            return pltpu.make_async_copy(rows_v.at[b], out_hbm.at[pos_v.at[k, j]], store_sem.at[b])

        for b in range(min(SC_BUFFERS, nchunk)):
            load(b, b).start()

        @pl.loop(0, nchunk, step=SC_BUFFERS)
        def _(g):
            for b in range(SC_BUFFERS):
                j = g + b

                @pl.when(j < nchunk)
                def _():
                    load(j, b).wait()
                    for k in range(kk):
                        store(j, b, k).start()
                    for k in range(kk):
                        store(j, b, k).wait()

                    @pl.when(j + SC_BUFFERS < nchunk)
                    def _():
                        load(j + SC_BUFFERS, b).start()

    return scatter_kernel(rows, pos.reshape(kk, nw, nchunk, SC_CHUNK).transpose(1, 0, 2, 3))


def _sc_gather_rows(table, idx):
    n = idx.shape[0]
    w = table.shape[1]
    nc, nw = _sc_workers()
    per_w = n // nw
    nchunk = per_w // SC_CHUNK
    mesh = plsc.VectorSubcoreMesh(core_axis_name="c", subcore_axis_name="s")

    @functools.partial(
        pl.kernel, mesh=mesh, out_type=jax.ShapeDtypeStruct((n, w), table.dtype),
        scratch_types=[pltpu.VMEM((nchunk, SC_CHUNK), jnp.int32), pltpu.VMEM((SC_BUFFERS, SC_CHUNK, w), table.dtype),
                       pltpu.SemaphoreType.DMA((SC_BUFFERS,)), pltpu.SemaphoreType.DMA((SC_BUFFERS,))],
        name="moe_collect")
    def gather_kernel(table_hbm, idx_hbm, out_hbm, idx_v, rows_v, gather_sem, store_sem):
        wid = lax.axis_index("s") * nc + lax.axis_index("c")
        base = wid * per_w
        pltpu.sync_copy(idx_hbm.at[wid], idx_v)

        def gather(j, b):
            return pltpu.make_async_copy(table_hbm.at[idx_v.at[j]], rows_v.at[b], gather_sem.at[b])

        def store(j, b):
            dst = out_hbm.at[pl.ds(pl.multiple_of(base + j * SC_CHUNK, SC_CHUNK), SC_CHUNK)]
            return pltpu.make_async_copy(rows_v.at[b], dst, store_sem.at[b])

        for b in range(min(SC_BUFFERS, nchunk)):
            gather(b, b).start()

        @pl.loop(0, nchunk, step=SC_BUFFERS)
        def _(g):
            for b in range(SC_BUFFERS):
                j = g + b

                @pl.when(j < nchunk)
                def _():
                    gather(j, b).wait()
                    store(j, b).start()
                    store(j, b).wait()

                    @pl.when(j + SC_BUFFERS < nchunk)
                    def _():
                        gather(j + SC_BUFFERS, b).start()

    return gather_kernel(table, idx.reshape(nw, nchunk, SC_CHUNK))


def _experts_kernel(te_ref, nu_ref, xs_ref, wg_ref, wu_ref, wd_ref, ys_ref, h_scr, acc_ref):
    i = pl.program_id(0)
    j = pl.program_id(1)

    @pl.when(i < nu_ref[0])
    def _():
        @pl.when(j == 0)
        def _():
            lo, hi = _unpack_pair(xs_ref[...])
            half = lo.shape[1]
            h_scr[:, :half] = lo.astype(BF16)
            h_scr[:, half:] = hi.astype(BF16)
            acc_ref[...] = jnp.zeros_like(acc_ref)

        h = h_scr[...]
        a = _dot(h, wg_ref[...].astype(BF16))
        u = _dot(h, wu_ref[...].astype(BF16))
        acc_ref[...] += _dot((a * _sigmoid(a) * u).astype(BF16), wd_ref[...].astype(BF16))

        @pl.when(j == pl.num_programs(1) - 1)
        def _():
            ys_ref[...] = _pack_pair(acc_ref[...])


def _experts(xs, tile_expert, n_used, wg, wu, wd, layer, *, tm, tf=FFN_TILE):
    r, half = xs.shape
    d = 2 * half
    f = wg.shape[3]
    grid_spec = pltpu.PrefetchScalarGridSpec(
        num_scalar_prefetch=2,
        grid=(r // tm, f // tf),
        in_specs=[pl.BlockSpec((tm, half), lambda i, j, te, nu: (i, 0)),
                  pl.BlockSpec((None, None, d, tf), lambda i, j, te, nu: (layer, te[i], 0, j)),
                  pl.BlockSpec((None, None, d, tf), lambda i, j, te, nu: (layer, te[i], 0, j)),
                  pl.BlockSpec((None, None, tf, d), lambda i, j, te, nu: (layer, te[i], j, 0))],
        out_specs=pl.BlockSpec((tm, half), lambda i, j, te, nu: (i, 0)),
        scratch_shapes=[pltpu.VMEM((tm, d), BF16), pltpu.VMEM((tm, d), F32)])
    return pl.pallas_call(
        _experts_kernel,
        grid_spec=grid_spec,
        out_shape=jax.ShapeDtypeStruct((r, half), jnp.int32),
        compiler_params=_cparams(("arbitrary", "arbitrary")),
        name="moe_experts",
    )(tile_expert, n_used, xs, wg, wu, wd)


def _moe_out_kernel(*refs, final_norm):
    if final_norm:
        x_ref, y1_ref, y2_ref, meta_ref, gf_ref, o_ref = refs
    else:
        x_ref, y1_ref, y2_ref, meta_ref, o_ref = refs
    meta = meta_ref[...]
    w1, w2 = meta[:, 0:1], meta[:, 1:2]
    lo1, hi1 = _unpack_pair(y1_ref[...])
    lo2, hi2 = _unpack_pair(y2_ref[...])
    half = lo1.shape[1]
    x = x_ref[...]
    y = jnp.concatenate([x[:, :half] + w1 * lo1 + w2 * lo2, x[:, half:] + w1 * hi1 + w2 * hi2], axis=1)
    if final_norm:
        y = _rms(y, gf_ref[...])
    o_ref[...] = y


def _moe_out(x2, yg, meta, gf, *, tm=512):
    t, d = x2.shape
    tm = min(tm, t)
    nblk = t // tm
    final_norm = gf is not None
    in_specs = [pl.BlockSpec((tm, d), lambda i: (i, 0)),
                pl.BlockSpec((tm, d // 2), lambda i: (i, 0)),
                pl.BlockSpec((tm, d // 2), lambda i: (i + nblk, 0)),
                pl.BlockSpec((tm, LANES), lambda i: (i, 0))]
    args = [x2, yg, yg, meta]
    if final_norm:
        in_specs.append(pl.BlockSpec((1, d), lambda i: (0, 0)))
        args.append(gf.reshape(1, d))
    return pl.pallas_call(
        functools.partial(_moe_out_kernel, final_norm=final_norm),
        grid=(nblk,),
        in_specs=in_specs,
        out_specs=pl.BlockSpec((tm, d), lambda i: (i, 0)),
        out_shape=jax.ShapeDtypeStruct((t, d), F32),
        compiler_params=_cparams(("parallel",)),
        name="moe_out",
    )(*args)


def _moe(h2p, xn, meta, totals, wg, wu, wd, layer, gf, *, tm=MOE_TILE):
    t = xn.shape[0]
    tm = min(tm, t)
    ne = wg.shape[1]
    cnt = totals[0, :ne].astype(jnp.int32)
    cap = ((cnt + tm - 1) // tm) * tm
    ends = jnp.cumsum(cap)
    off = ends - cap
    n_tiles = 2 * t // tm + ne
    tile_start = jnp.arange(n_tiles, dtype=jnp.int32) * tm
    tile_expert = jnp.minimum(jnp.sum(tile_start[:, None] >= ends[None, :], axis=1), ne - 1).astype(jnp.int32)
    n_used = (ends[-1:] // tm).astype(jnp.int32)
    routing = meta[:, 2:6].T.astype(jnp.int32)
    pos = off[routing[0:2]] + routing[2:4]
    xs = _sc_scatter_rows(h2p, pos, n_tiles * tm)
    ys = _experts(xs, tile_expert, n_used, wg, wu, wd, layer, tm=tm)
    yg = _sc_gather_rows(ys, pos.reshape(2 * t))
    return _moe_out(xn, yg, meta, gf)


def _final_norm_kernel(x_ref, g_ref, o_ref):
    o_ref[...] = _rms(x_ref[...], g_ref[...])


def _final_norm(x2, g, *, tm=512):
    t, d = x2.shape
    tm = min(tm, t)
    return pl.pallas_call(
        _final_norm_kernel,
        grid=(t // tm,),
        in_specs=[pl.BlockSpec((tm, d), lambda i: (i, 0)), pl.BlockSpec((1, d), lambda i: (0, 0))],
        out_specs=pl.BlockSpec((tm, d), lambda i: (i, 0)),
        out_shape=jax.ShapeDtypeStruct((t, d), F32),
        compiler_params=_cparams(("parallel",)),
        name="final_norm",
    )(x2, g.reshape(1, d))


def kernel(x, mix_norm, w_in, hgrn_lb_logits, hgrn_out_norm, w_branch_hgrn, w_branch_sb, w_out, ffn_norm,
           dense_w_gate, dense_w_up, dense_w_down, moe_router, moe_w_gate, moe_w_up, moe_w_down, final_norm):
    b, s, d = x.shape
    t = b * s
    depth = w_in.shape[0]
    hgrn_width = HGRN_HEADS * HGRN_HEAD_DIM
    sb_width = SB_HEADS * SB_HEAD_DIM
    hgrn_plane = 0
    sb_plane = hgrn_plane + 4 * hgrn_width // LANES
    gate_plane = sb_plane + 3 * sb_width // LANES

    lb_all = jnp.cumsum(jax.nn.softmax(hgrn_lb_logits.astype(F32), axis=0), axis=0)
    lb_all = lb_all - lb_all[0:1]
    lb_params = jnp.stack([jnp.log(lb_all), jnp.log1p(-lb_all), 1.0 - lb_all], axis=1)
    lb_params = lb_params.reshape(depth, 3, HGRN_HEADS, HGRN_HEAD_DIM).transpose(0, 2, 1, 3)

    w_in, w_branch_hgrn, w_branch_sb, w_out = (w.astype(BF16) for w in (w_in, w_branch_hgrn, w_branch_sb, w_out))

    x2 = x.reshape(t, d)
    for layer in range(depth):
        planes = _inproj(x2, mix_norm[layer], w_in, layer)
        planes4 = planes.reshape(planes.shape[0], b, s, LANES)
        oa = _hgrn(planes4, lb_params[layer], hgrn_out_norm[layer], first_plane=hgrn_plane)
        ob = _sb_attention(planes4, first_plane=sb_plane)
        j = layer // 2
        moe = layer % 2 == 1
        wr = None
        if moe:
            wr = jnp.zeros((d, LANES), F32).at[:, :N_EXPERTS].set(moe_router[j].astype(F32))
            wr_hi = wr.astype(BF16)
            wr = jnp.concatenate([wr_hi, (wr - wr_hi.astype(F32)).astype(BF16)], axis=1)
        res = _combine(x2, oa.reshape(HGRN_HEADS, t, LANES), ob.reshape(ob.shape[0], t, LANES), planes,
                       w_branch_hgrn, w_branch_sb, w_out, layer, gate_plane=gate_plane,
                       g2=ffn_norm[layer] if moe else None, wr=wr)
        last = layer == depth - 1
        if moe:
            xn, h2p, meta, totals = res
            x2 = _moe(h2p, xn, meta, totals, moe_w_gate, moe_w_up, moe_w_down, j, final_norm if last else None)
        else:
            (xn,) = res
            x2 = _swiglu(xn, ffn_norm[layer], dense_w_gate, dense_w_up, dense_w_down, j)
            if last:
                x2 = _final_norm(x2, final_norm)
    return x2.reshape(b, s, d)
```
